```python
import math
import jax, jax.numpy as jnp
from jax import lax
import numpy as np

D_MODEL = 1024
BATCH = 4
SEQ = 8192
DEPTH = 2

NORM_EPS = 1e-6
D_FF = 2816
D_MIX = D_MODEL
MLA_HEADS = 8
MLA_NOPE_DIM = 64
MLA_ROPE_DIM = 32
MLA_V_DIM = 64
MLA_QK_DIM = MLA_NOPE_DIM + MLA_ROPE_DIM
MLA_Q_RANK = 256
MLA_KV_RANK = 128
MLA_WIDTH = MLA_HEADS * MLA_V_DIM
ROPE_THETA = 10000.0
Q_BLOCK = 128
CONV_WIDTH = D_MIX // 4
CONV_KERNEL = 31
HY_WIDTH = D_MIX - MLA_WIDTH - CONV_WIDTH
HY_ORDER = 2
HY_SHORT_KERNEL = 3
HY_EMB_DIM = 33
HY_FILTER_DIM = 64
HY_FAST_DECAY_PCT = 0.3
HY_SLOW_DECAY_PCT = 1.5
HY_DECAY_TARGET = 1e-2
HY_FILTER_OUT = HY_ORDER * 2 * HY_WIDTH
IN_MLA_Q = MLA_Q_RANK
IN_MLA_KV = MLA_KV_RANK
IN_MLA_KPE = MLA_ROPE_DIM
IN_CONV = 2 * CONV_WIDTH
IN_HY = (HY_ORDER + 1) * HY_WIDTH
OFF_Q = 0
OFF_KV = OFF_Q + IN_MLA_Q
OFF_KPE = OFF_KV + IN_MLA_KV
OFF_CONV = OFF_KPE + IN_MLA_KPE
OFF_HY = OFF_CONV + IN_CONV
IN_COLS = OFF_HY + IN_HY

kernel_name = "hymba_mla_conformer_hyena_macaron"


def rmsnorm(x, g):
    xf = x.astype(jnp.float32)
    xf = xf * lax.rsqrt(jnp.mean(xf * xf, axis=-1, keepdims=True) + NORM_EPS)
    return (xf * g.astype(jnp.float32)).astype(x.dtype)


def layernorm(x, g, b):
    xf = x.astype(jnp.float32)
    mu = jnp.mean(xf, axis=-1, keepdims=True)
    var = jnp.mean(jnp.square(xf - mu), axis=-1, keepdims=True)
    y = (xf - mu) * lax.rsqrt(var + NORM_EPS)
    return (y * g.astype(jnp.float32) + b.astype(jnp.float32)).astype(x.dtype)


def swiglu(h, w_gate, w_up, w_down):
    return (jax.nn.silu(h @ w_gate) * (h @ w_up)) @ w_down


def depthwise_conv(u, w, b):
    k, c = w.shape
    pad = k // 2
    y = lax.conv_general_dilated(u, w[:, None, :].astype(u.dtype), window_strides=(1,),
                                 padding=[(pad, pad)], dimension_numbers=('NWC', 'WIO', 'NWC'),
                                 feature_group_count=c)
    return y + b.astype(u.dtype)


def rope_cos_sin(positions, dtype):
    inv_freq = 1.0 / (ROPE_THETA ** (jnp.arange(0, MLA_ROPE_DIM, 2, dtype=jnp.float32) / MLA_ROPE_DIM))
    ang = positions.astype(jnp.float32)[..., None] * inv_freq
    return jnp.cos(ang).astype(dtype), jnp.sin(ang).astype(dtype)


def apply_rope(x, cos, sin):
    x1, x2 = jnp.split(x, 2, axis=-1)
    return jnp.concatenate([x1 * cos - x2 * sin, x1 * sin + x2 * cos], axis=-1)


def mla_mixer(h_q, h_kv, k_pe, positions, q_norm, w_qb, kv_norm, w_kvb):
    b, l, _ = h_q.shape
    cos, sin = rope_cos_sin(positions, h_q.dtype)
    q = (rmsnorm(h_q, q_norm) @ w_qb).reshape(b, l, MLA_HEADS, MLA_QK_DIM)
    q_pe = apply_rope(q[..., MLA_NOPE_DIM:], cos[:, :, None], sin[:, :, None])
    q = jnp.concatenate([q[..., :MLA_NOPE_DIM], q_pe], axis=-1) * (MLA_QK_DIM ** -0.5)
    kv = (rmsnorm(h_kv, kv_norm) @ w_kvb).reshape(b, l, MLA_HEADS, MLA_NOPE_DIM + MLA_V_DIM)
    k_pe = apply_rope(k_pe, cos, sin)
    k = jnp.concatenate([kv[..., :MLA_NOPE_DIM],
                         jnp.broadcast_to(k_pe[:, :, None], (b, l, MLA_HEADS, MLA_ROPE_DIM))], axis=-1)
    v = kv[..., MLA_NOPE_DIM:]
    n_blk = l // Q_BLOCK
    q_blocks = q.reshape(b, n_blk, Q_BLOCK, MLA_HEADS, MLA_QK_DIM).transpose(1, 0, 2, 3, 4)

    def attend(qb):
        s = jnp.einsum('bqhd,bkhd->bhqk', qb, k).astype(jnp.float32)
        p = jax.nn.softmax(s, axis=-1).astype(v.dtype)
        return jnp.einsum('bhqk,bkhd->bqhd', p, v)

    o = lax.map(attend, q_blocks)
    return o.transpose(1, 0, 2, 3, 4).reshape(b, l, MLA_WIDTH)


def conformer_conv_mixer(h_conv, dw_w, dw_b, ln_g, ln_b):
    a, g = jnp.split(h_conv, 2, axis=-1)
    u = a * jax.nn.sigmoid(g)
    u = depthwise_conv(u, dw_w, dw_b)
    return jax.nn.silu(layernorm(u, ln_g, ln_b))


def hyena_filters(l, w1, b1, w2, b2, w3, b3, w4, freq):
    f32 = jnp.float32
    t = jnp.linspace(0.0, 1.0, l, dtype=f32)[:, None]
    bands = (HY_EMB_DIM - 1) // 2
    ang = 2.0 * math.pi * jnp.arange(l, dtype=f32)[:, None] / l
    fb = jnp.linspace(1e-4, bands - 1, bands, dtype=f32)[None, :]
    z = jnp.concatenate([t, jnp.cos(fb * ang), -jnp.sin(fb * ang)], axis=-1)
    fr = freq.astype(f32)
    hdn = jnp.sin(fr * (z @ w1.astype(f32) + b1.astype(f32)))
    hdn = jnp.sin(fr * (hdn @ w2.astype(f32) + b2.astype(f32)))
    hdn = jnp.sin(fr * (hdn @ w3.astype(f32) + b3.astype(f32)))
    h = (hdn @ w4.astype(f32)).reshape(l, HY_ORDER, 2, HY_WIDTH)
    max_decay = math.log(HY_DECAY_TARGET) / HY_FAST_DECAY_PCT
    min_decay = math.log(HY_DECAY_TARGET) / HY_SLOW_DECAY_PCT
    deltas = jnp.linspace(min_decay, max_decay, HY_WIDTH, dtype=f32)
    h = h * jnp.exp(-t * jnp.abs(deltas))[:, None, None, :]
    fwd, bwd = h[:, :, 0], h[:, :, 1]
    k = jnp.concatenate([fwd, jnp.zeros_like(fwd[:1]), jnp.flip(bwd[1:], axis=0)], axis=0)
    return k / jnp.sum(jnp.abs(k), axis=0, keepdims=True)


def fft_long_conv(u, k, d):
    l = u.shape[1]
    uf = u.astype(jnp.float32)
    uk = jnp.fft.rfft(uf, n=2 * l, axis=1) * jnp.fft.rfft(k, axis=0)[None]
    y = jnp.fft.irfft(uk, n=2 * l, axis=1)[:, :l]
    return (y + uf * d.astype(jnp.float32)).astype(u.dtype)


def hyena_mixer(h_hy, short_w, short_b, w1, b1, w2, b2, w3, b3, w4, freq, bias_d):
    u = depthwise_conv(h_hy, short_w, short_b)
    v, x1, x2 = jnp.split(u, HY_ORDER + 1, axis=-1)
    k = hyena_filters(h_hy.shape[1], w1, b1, w2, b2, w3, b3, w4, freq)
    z = x1 * fft_long_conv(v, k[:, 0], bias_d[0])
    return x2 * fft_long_conv(z, k[:, 1], bias_d[1])


def setup_inputs(seed: int = 0) -> dict:
    key = jax.random.key(seed)
    ks = iter(jax.random.split(key, 64))
    f32 = jnp.float32

    def nrm(shape, scale):
        return jax.random.normal(next(ks), shape, f32) * scale

    def gain(shape):
        return 1.0 + nrm(shape, 0.05)

    dd = DEPTH
    return {
        "x": nrm((BATCH, SEQ, D_MODEL), 1.0),
        "positions": jnp.broadcast_to(jnp.arange(SEQ, dtype=jnp.int32)[None], (BATCH, SEQ)),
        "ffn1_norm": gain((dd, D_MODEL)),
        "ffn1_w_gate": nrm((dd, D_MODEL, D_FF), D_MODEL ** -0.5),
        "ffn1_w_up": nrm((dd, D_MODEL, D_FF), D_MODEL ** -0.5),
        "ffn1_w_down": nrm((dd, D_FF, D_MODEL), D_FF ** -0.5),
        "mix_norm": gain((dd, D_MODEL)),
        "w_in": nrm((dd, D_MODEL, IN_COLS), D_MODEL ** -0.5),
        "mla_q_norm": gain((dd, MLA_Q_RANK)),
        "mla_w_qb": nrm((dd, MLA_Q_RANK, MLA_HEADS * MLA_QK_DIM), MLA_Q_RANK ** -0.5),
        "mla_kv_norm": gain((dd, MLA_KV_RANK)),
        "mla_w_kvb": nrm((dd, MLA_KV_RANK, MLA_HEADS * (MLA_NOPE_DIM + MLA_V_DIM)), MLA_KV_RANK ** -0.5),
        "conv_dw_w": nrm((dd, CONV_KERNEL, CONV_WIDTH), CONV_KERNEL ** -0.5),
        "conv_dw_b": nrm((dd, CONV_WIDTH), 0.02),
        "conv_ln_g": gain((dd, CONV_WIDTH)),
        "conv_ln_b": nrm((dd, CONV_WIDTH), 0.02),
        "hy_short_w": nrm((dd, HY_SHORT_KERNEL, IN_HY), HY_SHORT_KERNEL ** -0.5),
        "hy_short_b": nrm((dd, IN_HY), 0.02),
        "hy_filt_w1": nrm((dd, HY_EMB_DIM, HY_FILTER_DIM), HY_EMB_DIM ** -0.5),
        "hy_filt_b1": nrm((dd, HY_FILTER_DIM), 0.1),
        "hy_filt_w2": nrm((dd, HY_FILTER_DIM, HY_FILTER_DIM), HY_FILTER_DIM ** -0.5),
        "hy_filt_b2": nrm((dd, HY_FILTER_DIM), 0.1),
        "hy_filt_w3": nrm((dd, HY_FILTER_DIM, HY_FILTER_DIM), HY_FILTER_DIM ** -0.5),
        "hy_filt_b3": nrm((dd, HY_FILTER_DIM), 0.1),
        "hy_filt_w4": nrm((dd, HY_FILTER_DIM, HY_FILTER_OUT), HY_FILTER_DIM ** -0.5),
        "hy_filt_freq": gain((dd, HY_FILTER_DIM)),
        "hy_bias_d": nrm((dd, HY_ORDER, HY_WIDTH), 0.5),
        "out_norm": gain((dd, D_MIX)),
        "w_out": nrm((dd, D_MIX, D_MODEL), D_MIX ** -0.5),
        "ffn2_norm": gain((dd, D_MODEL)),
        "ffn2_w_gate": nrm((dd, D_MODEL, D_FF), D_MODEL ** -0.5),
        "ffn2_w_up": nrm((dd, D_MODEL, D_FF), D_MODEL ** -0.5),
        "ffn2_w_down": nrm((dd, D_FF, D_MODEL), D_FF ** -0.5),
        "final_norm": gain((D_MODEL,)),
    }


def reference(x, positions, ffn1_norm, ffn1_w_gate, ffn1_w_up, ffn1_w_down, mix_norm, w_in,
              mla_q_norm, mla_w_qb, mla_kv_norm, mla_w_kvb, conv_dw_w, conv_dw_b, conv_ln_g, conv_ln_b,
              hy_short_w, hy_short_b, hy_filt_w1, hy_filt_b1, hy_filt_w2, hy_filt_b2, hy_filt_w3, hy_filt_b3,
              hy_filt_w4, hy_filt_freq, hy_bias_d, out_norm, w_out, ffn2_norm, ffn2_w_gate, ffn2_w_up,
              ffn2_w_down, final_norm):
    e1 = MLA_WIDTH
    e2 = MLA_WIDTH + CONV_WIDTH
    for i in range(DEPTH):
        x = x + 0.5 * swiglu(rmsnorm(x, ffn1_norm[i]), ffn1_w_gate[i], ffn1_w_up[i], ffn1_w_down[i])
        h = rmsnorm(x, mix_norm[i]) @ w_in[i]
        y_mla = mla_mixer(h[..., OFF_Q:OFF_KV], h[..., OFF_KV:OFF_KPE], h[..., OFF_KPE:OFF_CONV], positions,
                          mla_q_norm[i], mla_w_qb[i], mla_kv_norm[i], mla_w_kvb[i])
        y_conv = conformer_conv_mixer(h[..., OFF_CONV:OFF_HY], conv_dw_w[i], conv_dw_b[i],
                                      conv_ln_g[i], conv_ln_b[i])
        y_hy = hyena_mixer(h[..., OFF_HY:], hy_short_w[i], hy_short_b[i], hy_filt_w1[i], hy_filt_b1[i],
                           hy_filt_w2[i], hy_filt_b2[i], hy_filt_w3[i], hy_filt_b3[i], hy_filt_w4[i],
                           hy_filt_freq[i], hy_bias_d[i])
        g = out_norm[i]
        y = jnp.concatenate([rmsnorm(y_mla, g[:e1]), rmsnorm(y_conv, g[e1:e2]), rmsnorm(y_hy, g[e2:])], axis=-1)
        x = x + y @ w_out[i]
        x = x + 0.5 * swiglu(rmsnorm(x, ffn2_norm[i]), ffn2_w_gate[i], ffn2_w_up[i], ffn2_w_down[i])
    return rmsnorm(x, final_norm)
```

```python
import functools
import math

import numpy as np
import jax
import jax.numpy as jnp
from jax import lax
from jax.experimental import pallas as pl
from jax.experimental.pallas import tpu as pltpu

F32 = jnp.float32
BF16 = jnp.bfloat16

D_MODEL = 1024
NORM_EPS = 1e-6
D_FF = 2816
HEADS = 8
NOPE = 64
ROPE = 32
VDIM = 64
QK_DIM = NOPE + ROPE
Q_RANK = 256
KV_RANK = 128
MLA_WIDTH = HEADS * VDIM
ROPE_THETA = 10000.0
CONV_WIDTH = 256
CONV_KERNEL = 31
HY_WIDTH = 256
HY_ORDER = 2
HY_SHORT = 3
HY_EMB = 33
HY_HID = 64
HY_FAST, HY_SLOW, HY_TARGET = 0.3, 1.5, 1e-2
OFF_Q = 0
OFF_KV = OFF_Q + Q_RANK
OFF_KPE = OFF_KV + KV_RANK
OFF_CONV = OFF_KPE + ROPE
OFF_HY = OFF_CONV + 2 * CONV_WIDTH
IN_COLS = OFF_HY + 3 * HY_WIDTH
LOC_COLS = 2 * CONV_WIDTH + 3 * HY_WIDTH

LANE = 128
HEAD_PAD = LANE
DFT_MINOR = 128
HALO = 16
VMEM_LIMIT = 56 * 1024 * 1024


def _cparams(sem):
    return pltpu.CompilerParams(dimension_semantics=sem, vmem_limit_bytes=VMEM_LIMIT)


def _const_spec(shape):
    nd = len(shape)
    return pl.BlockSpec(shape, lambda *_: (0,) * nd, pipeline_mode=pl.Buffered(1))


def _rms(x, g):
    ms = jnp.mean(x * x, axis=-1, keepdims=True)
    return x * lax.rsqrt(ms + NORM_EPS) * g


def _dot(a, b):
    return jnp.dot(a, b, preferred_element_type=F32)


def _ffn_body(*refs, mix, final, ff_chunk):
    refs = list(refs)
    x_ref = refs.pop(0)
    if mix:
        ym_ref, yc_ref, yh_ref, x2_ref, on_ref, wo_ref = refs[:6]
        refs = refs[6:]
    g_ref, wg_ref, wu_ref, wd_ref = refs[:4]
    refs = refs[4:]
    if final:
        fin_ref = refs.pop(0)
    o_ref, a_scr = refs

    x = x_ref[...]
    if mix:
        on = on_ref[...]
        e1, e2 = MLA_WIDTH, MLA_WIDTH + CONV_WIDTH
        y = jnp.concatenate([
            _rms(ym_ref[...], on[:, :e1]),
            _rms(yc_ref[...], on[:, e1:e2]),
            _rms(yh_ref[...] * x2_ref[...], on[:, e2:]),
        ], axis=-1).astype(BF16)
        x = x + _dot(y, wo_ref[...])
    h = _rms(x, g_ref[...]).astype(BF16)
    for c in range(D_FF // ff_chunk):
        sl = slice(c * ff_chunk, (c + 1) * ff_chunk)
        g = _dot(h, wg_ref[:, sl])
        u = _dot(h, wu_ref[:, sl])
        a_scr[:, sl] = (g * jax.nn.sigmoid(g) * u).astype(BF16)
    y = x + 0.5 * _dot(a_scr[...], wd_ref[...])
    if final:
        y = _rms(y, fin_ref[...])
    o_ref[...] = y


def _ffn_call(x, norm_g, wg, wu, wd, mix=None, final_g=None, tm=512, ff_chunk=256):
    n, d = x.shape
    tm = min(tm, n)
    row = lambda w: pl.BlockSpec((tm, w), lambda i: (i, 0))
    args, specs = [x], [row(d)]
    if mix is not None:
        ym, yc, yh, x2, on, wo = mix
        args += [ym, yc, yh, x2, on, wo]
        specs += [row(ym.shape[1]), row(yc.shape[1]), row(yh.shape[1]), row(x2.shape[1]),
                  _const_spec(on.shape), _const_spec(wo.shape)]
    args += [norm_g, wg, wu, wd]
    specs += [_const_spec(norm_g.shape), _const_spec(wg.shape), _const_spec(wu.shape), _const_spec(wd.shape)]
    if final_g is not None:
        args.append(final_g)
        specs.append(_const_spec(final_g.shape))
    return pl.pallas_call(
        functools.partial(_ffn_body, mix=mix is not None, final=final_g is not None, ff_chunk=ff_chunk),
        grid=(n // tm,),
        in_specs=specs,
        out_specs=row(d),
        out_shape=jax.ShapeDtypeStruct((n, d), F32),
        scratch_shapes=[pltpu.VMEM((tm, D_FF), BF16)],
        compiler_params=_cparams(("parallel",)),
        name="ffn_mix" if mix is not None else "ffn",
    )(*args)


def _proj_body(x_ref, pos_ref, g_ref, win_ref, qn_ref, wqa_ref, wqb_ref, kvn_ref, wk_ref, wv_ref,
               freq_ref, sign_ref, vone_ref, q_ref, k_ref, v_ref, loc_ref):
    xn = _rms(x_ref[...], g_ref[...]).astype(BF16)
    h = _dot(xn, win_ref[...])
    ang = pos_ref[...] * freq_ref[...]
    cos = jnp.cos(ang)
    sin = jnp.sin(ang) * sign_ref[...]
    o1 = Q_RANK
    o2 = o1 + KV_RANK
    o3 = o2 + HEAD_PAD
    o4 = o3 + HEAD_PAD
    cq = _rms(h[:, :o1], qn_ref[...]).astype(BF16)
    ckv = _rms(h[:, o1:o2], kvn_ref[...]).astype(BF16)
    kpe = h[:, o2:o3] * cos + h[:, o3:o4] * sin
    loc_ref[...] = h[:, o4:]
    qa = _dot(cq, wqa_ref[...])
    qb = _dot(cq, wqb_ref[...])
    kk = _dot(ckv, wk_ref[...])
    vv = _dot(ckv, wv_ref[...])
    scale = QK_DIM ** -0.5
    cs, ss = cos * scale, sin * scale
    vone = vone_ref[...]
    for hd in range(HEADS):
        sl = slice(hd * HEAD_PAD, (hd + 1) * HEAD_PAD)
        q_ref[0, hd] = (qa[:, sl] * cs + qb[:, sl] * ss).astype(BF16)
        k_ref[0, hd] = (kk[:, sl] + kpe).astype(BF16)
        v_ref[0, hd] = (vv[:, sl] + vone).astype(BF16)


def _proj_call(x, pos, b, l, g, win, qn, wqa, wqb, kvn, wk, wv, freq, sign, vone, tm=512):
    n, d = x.shape
    tm = min(tm, l)
    nlt = l // tm
    head_spec = pl.BlockSpec((1, HEADS, tm, HEAD_PAD), lambda i: (i // nlt, 0, i % nlt, 0))
    head_shape = jax.ShapeDtypeStruct((b, HEADS, l, HEAD_PAD), BF16)
    consts = [g, win, qn, wqa, wqb, kvn, wk, wv, freq, sign, vone]
    return pl.pallas_call(
        _proj_body,
        grid=(n // tm,),
        in_specs=[pl.BlockSpec((tm, d), lambda i: (i, 0)), pl.BlockSpec((tm, 1), lambda i: (i, 0))]
        + [_const_spec(c.shape) for c in consts],
        out_specs=[head_spec, head_spec, head_spec, pl.BlockSpec((tm, LOC_COLS), lambda i: (i, 0))],
        out_shape=[head_shape, head_shape, head_shape, jax.ShapeDtypeStruct((n, LOC_COLS), F32)],
        compiler_params=_cparams(("parallel",)),
        name="in_proj",
    )(x, pos, *consts)


def _attn_body(q_ref, k_ref, v_ref, o_ref, *, tk, heads_per_step):
    l = k_ref.shape[2]
    tq = q_ref.shape[2]
    outs = []
    for hd in range(heads_per_step):
        q = q_ref[0, hd]

        def step(j, carry, hd=hd, q=q):
            m, acc = carry
            off = pl.multiple_of(j * tk, tk)
            k = k_ref[0, hd, pl.ds(off, tk), :]
            v = v_ref[0, hd, pl.ds(off, tk), :]
            s = lax.dot_general(q, k, (((1,), (1,)), ((), ())), preferred_element_type=F32)
            m_new = jnp.maximum(m, jnp.max(s, axis=-1, keepdims=True))
            alpha = jnp.exp(m - m_new)
            p = jnp.exp(s - m_new).astype(BF16)
            return m_new, alpha * acc + _dot(p, v)

        m0 = jnp.full((tq, 1), -1e30, F32)
        acc0 = jnp.zeros((tq, HEAD_PAD), F32)
        _, acc = lax.fori_loop(0, l // tk, step, (m0, acc0))
        outs.append(acc[:, :VDIM] / acc[:, VDIM:VDIM + 1])
    o_ref[0] = jnp.concatenate(outs, axis=-1)


def _attn_call(q, k, v, tq=512, tk=512, heads_per_step=2):
    b, hn, l, dp = q.shape
    tq, tk = min(tq, l), min(tk, l)
    hps = heads_per_step
    return pl.pallas_call(
        functools.partial(_attn_body, tk=tk, heads_per_step=hps),
        grid=(b, hn // hps, l // tq),
        in_specs=[pl.BlockSpec((1, hps, tq, dp), lambda bi, hi, qi: (bi, hi, qi, 0)),
                  pl.BlockSpec((1, hps, l, dp), lambda bi, hi, qi: (bi, hi, 0, 0)),
                  pl.BlockSpec((1, hps, l, dp), lambda bi, hi, qi: (bi, hi, 0, 0))],
        out_specs=pl.BlockSpec((1, tq, hps * VDIM), lambda bi, hi, qi: (bi, qi, hi)),
        out_shape=jax.ShapeDtypeStruct((b, l, hn * VDIM), F32),
        compiler_params=_cparams(("parallel", "parallel", "arbitrary")),
        name="mla_attn",
    )(q, k, v)


def _local_body(cur_ref, prev_ref, next_ref, cw_ref, cb_ref, lg_ref, lb_ref, sw_ref, sb_ref,
                yc_ref, hv_ref, hx1_ref, hx2_ref, u_scr, s_scr):
    t = cur_ref.shape[1]
    i = pl.program_id(1)
    first = i == 0
    last = i == pl.num_programs(1) - 1
    cw = 2 * CONV_WIDTH

    def glu(w):
        return w[:, :CONV_WIDTH] * jax.nn.sigmoid(w[:, CONV_WIDTH:cw])

    prev = prev_ref[0]
    nxt = next_ref[0]
    cur = cur_ref[0]
    pmask = jnp.where(first, 0.0, 1.0)
    nmask = jnp.where(last, 0.0, 1.0)
    u_scr[0:HALO, :] = glu(prev) * pmask
    u_scr[HALO:HALO + t, :] = glu(cur)
    u_scr[HALO + t:, :] = glu(nxt) * nmask
    s_scr[0:HALO, :] = prev[:, cw:] * pmask
    s_scr[HALO:HALO + t, :] = cur[:, cw:]
    s_scr[HALO + t:, :] = nxt[:, cw:] * nmask

    pad = CONV_KERNEL // 2
    acc = jnp.zeros((t, CONV_WIDTH), F32) + cb_ref[...]
    for kk in range(CONV_KERNEL):
        acc = acc + u_scr[pl.ds(HALO - pad + kk, t), :] * cw_ref[kk:kk + 1, :]
    mu = jnp.mean(acc, axis=-1, keepdims=True)
    cen = acc - mu
    var = jnp.mean(cen * cen, axis=-1, keepdims=True)
    yn = cen * lax.rsqrt(var + NORM_EPS) * lg_ref[...] + lb_ref[...]
    yc_ref[0] = yn * jax.nn.sigmoid(yn)

    sp = HY_SHORT // 2
    hs = jnp.zeros((t, 3 * HY_WIDTH), F32) + sb_ref[...]
    for kk in range(HY_SHORT):
        hs = hs + s_scr[pl.ds(HALO - sp + kk, t), :] * sw_ref[kk:kk + 1, :]
    hv_ref[0] = hs[:, :HY_WIDTH]
    hx1_ref[0] = hs[:, HY_WIDTH:2 * HY_WIDTH]
    hx2_ref[0] = hs[:, 2 * HY_WIDTH:]


def _local_call(loc, cw, cb, lg, lb, sw, sb, t=512):
    b, l, c = loc.shape
    t = min(t, l)
    r = t // HALO
    nh = l // HALO
    consts = [cw, cb, lg, lb, sw, sb]
    out_spec = pl.BlockSpec((1, t, HY_WIDTH), lambda bi, i: (bi, i, 0))
    out_shape = jax.ShapeDtypeStruct((b, l, HY_WIDTH), F32)
    return pl.pallas_call(
        _local_body,
        grid=(b, l // t),
        in_specs=[pl.BlockSpec((1, t, c), lambda bi, i: (bi, i, 0)),
                  pl.BlockSpec((1, HALO, c), lambda bi, i: (bi, jnp.maximum(i * r - 1, 0), 0)),
                  pl.BlockSpec((1, HALO, c), lambda bi, i: (bi, jnp.minimum((i + 1) * r, nh - 1), 0))]
        + [_const_spec(a.shape) for a in consts],
        out_specs=[out_spec] * 4,
        out_shape=[out_shape] * 4,
        scratch_shapes=[pltpu.VMEM((t + 2 * HALO, CONV_WIDTH), F32),
                        pltpu.VMEM((t + 2 * HALO, 3 * HY_WIDTH), F32)],
        compiler_params=_cparams(("parallel", "parallel")),
        name="local_mixers",
    )(loc, loc, loc, *consts)


def _filter_body(z_ref, w1_ref, b1_ref, w2_ref, b2_ref, w3_ref, b3_ref, w4_ref, fr_ref, dl_ref,
                 k_ref, s_ref):
    hp = lax.Precision.HIGHEST
    z = z_ref[...]
    fr = fr_ref[...]
    hd = jnp.sin(fr * (jnp.dot(z, w1_ref[...], precision=hp, preferred_element_type=F32) + b1_ref[...]))
    hd = jnp.sin(fr * (jnp.dot(hd, w2_ref[...], precision=hp, preferred_element_type=F32) + b2_ref[...]))
    hd = jnp.sin(fr * (jnp.dot(hd, w3_ref[...], precision=hp, preferred_element_type=F32) + b3_ref[...]))
    h = jnp.dot(hd, w4_ref[0], precision=hp, preferred_element_type=F32)
    tcol = z[:, 0:1]
    mask = z[:, HY_EMB:HY_EMB + 1]
    win = jnp.exp(-tcol * dl_ref[...]) * mask

    @pl.when(pl.program_id(0) == 0)
    def _():
        s_ref[...] = jnp.zeros_like(s_ref)

    for o in range(HY_ORDER):
        ko = h[:, o * HY_WIDTH:(o + 1) * HY_WIDTH] * win
        k_ref[o] = ko
        s_ref[o] += jnp.broadcast_to(jnp.sum(jnp.abs(ko), axis=0, keepdims=True), s_ref.shape[1:])


def _filter_call(zt, w1, b1, w2, b2, w3, b3, w4, fr, dl, l, t=512):
    n2 = zt.shape[0]
    t = min(t, l)
    per_half = l // t
    consts_a = [w1, b1, w2, b2, w3, b3]
    return pl.pallas_call(
        _filter_body,
        grid=(n2 // t,),
        in_specs=[pl.BlockSpec((t, zt.shape[1]), lambda i: (i, 0))]
        + [_const_spec(a.shape) for a in consts_a]
        + [pl.BlockSpec((1,) + w4.shape[1:], lambda i: (i // per_half, 0, 0)),
           _const_spec(fr.shape), _const_spec(dl.shape)],
        out_specs=[pl.BlockSpec((HY_ORDER, t, HY_WIDTH), lambda i: (0, i, 0)),
                   pl.BlockSpec((HY_ORDER, 8, HY_WIDTH), lambda i: (0, 0, 0))],
        out_shape=[jax.ShapeDtypeStruct((HY_ORDER, n2, HY_WIDTH), F32),
                   jax.ShapeDtypeStruct((HY_ORDER, 8, HY_WIDTH), F32)],
        compiler_params=_cparams(("arbitrary",)),
        name="hyena_filter",
    )(zt, *consts_a, w4, fr, dl)


def _dft_dims(l):
    n1 = 2 * l // DFT_MINOR
    nh = n1 // 2
    h = nh + 1
    hp = -(-h // 8) * 8
    return n1, nh, h, hp


def _fwd_stage1(load_slab, tf1_ref, x1_scr, hp):
    def body(n2, c):
        u = load_slab(n2).astype(BF16)
        x1 = _dot(tf1_ref[n2], u)
        x1_scr[pl.ds(pl.multiple_of(n2 * 2 * hp, 8), 2 * hp), :] = x1
        return c
    lax.fori_loop(0, DFT_MINOR, body, 0)


def _fwd_stage2(k1, gf_ref, x1_scr, hp):
    are = x1_scr[pl.ds(k1, DFT_MINOR, stride=2 * hp), :]
    aim = x1_scr[pl.ds(hp + k1, DFT_MINOR, stride=2 * hp), :]
    rhs = jnp.concatenate([are, aim], axis=0).astype(BF16)
    return _dot(gf_ref[...], rhs)


def _spectrum_body(kf_ref, kb_ref, s_ref, tf1_ref, gf_ref, o_ref, x1_scr, *, l):
    _, nh, h, hp = _dft_dims(l)
    inv = 1.0 / s_ref[0, 0:1, :]

    for half, ref in enumerate((kf_ref, kb_ref)):
        _fwd_stage1(lambda n2, ref=ref: ref[0, pl.ds(n2, nh, stride=DFT_MINOR), :], tf1_ref, x1_scr, hp)

        def body(k1, c, half=half):
            x = _fwd_stage2(k1, gf_ref, x1_scr, hp) * inv
            if half == 0:
                o_ref[0, k1] = x
            else:
                sgn = (1 - 2 * (k1 % 2)).astype(F32)
                o_ref[0, k1] = o_ref[0, k1] + sgn * x
            return c
        lax.fori_loop(0, h, body, 0)


def _spectrum_call(kext, ssum, tf1, gf, l):
    _, nh, h, hp = _dft_dims(l)
    order, _, c = kext.shape
    cb = LANE
    return pl.pallas_call(
        functools.partial(_spectrum_body, l=l),
        grid=(order, c // cb),
        in_specs=[pl.BlockSpec((1, l, cb), lambda o, ci: (o, 0, ci), pipeline_mode=pl.Buffered(1)),
                  pl.BlockSpec((1, l, cb), lambda o, ci: (o, 1, ci), pipeline_mode=pl.Buffered(1)),
                  pl.BlockSpec((1, 8, cb), lambda o, ci: (o, 0, ci)),
                  _const_spec(tf1.shape), _const_spec(gf.shape)],
        out_specs=pl.BlockSpec((1, h, 2 * DFT_MINOR, cb), lambda o, ci: (o, 0, 0, ci),
                               pipeline_mode=pl.Buffered(1)),
        out_shape=jax.ShapeDtypeStruct((order, h, 2 * DFT_MINOR, c), F32),
        scratch_shapes=[pltpu.VMEM((DFT_MINOR * 2 * hp, cb), F32)],
        compiler_params=_cparams(("parallel", "parallel")),
        name="hyena_spectrum",
    )(kext, kext, ssum, tf1, gf)


def _longconv_body(*refs, gated, l):
    if gated:
        a_ref, b_ref, kf_ref, d_ref, tf1_ref, gf_ref, gi_ref, ti2_ref, o_ref, x1_scr, z_scr = refs
    else:
        a_ref, kf_ref, d_ref, tf1_ref, gf_ref, gi_ref, ti2_ref, o_ref, x1_scr, z_scr = refs
        b_ref = None
    _, nh, h, hp = _dft_dims(l)
    m = DFT_MINOR

    def load_slab(n2):
        u = a_ref[0, pl.ds(n2, nh, stride=m), :]
        if gated:
            u = u * b_ref[0, pl.ds(n2, nh, stride=m), :]
        return u

    _fwd_stage1(load_slab, tf1_ref, x1_scr, hp)

    def mid(k1, c):
        x = _fwd_stage2(k1, gf_ref, x1_scr, hp)
        xre, xim = x[:m], x[m:]
        kre = kf_ref[0, k1, 0:m, :]
        kim = kf_ref[0, k1, m:2 * m, :]
        y = jnp.concatenate([xre * kre - xim * kim, xre * kim + xim * kre], axis=0).astype(BF16)
        z_scr[pl.ds(pl.multiple_of(k1 * 2 * m, 2 * m), 2 * m), :] = _dot(gi_ref[...], y)
        return c
    lax.fori_loop(0, h, mid, 0)
    if hp > h:
        z_scr[h * 2 * m:, :] = jnp.zeros(((hp - h) * 2 * m, z_scr.shape[1]), F32)

    def last(n2, c):
        zre = z_scr[pl.ds(n2, hp, stride=2 * m), :]
        zim = z_scr[pl.ds(m + n2, hp, stride=2 * m), :]
        rhs = jnp.concatenate([zre, zim], axis=0).astype(BF16)
        y = _dot(ti2_ref[n2], rhs)
        o_ref[0, pl.ds(n2, nh, stride=m), :] = y
        return c
    lax.fori_loop(0, m, last, 0)

    rows = min(512, l)

    def epi(i, c):
        sl = pl.ds(pl.multiple_of(i * rows, rows), rows)
        u = a_ref[0, sl, :]
        if gated:
            u = u * b_ref[0, sl, :]
        o_ref[0, sl, :] = o_ref[0, sl, :] + u * d_ref[0]
        return c
    lax.fori_loop(0, l // rows, epi, 0)


def _longconv_call(a, b, kf, d, tf1, gf, gi, ti2, order):
    bn, l, c = a.shape
    _, nh, h, hp = _dft_dims(l)
    cb = LANE
    sig = pl.BlockSpec((1, l, cb), lambda ci, bi: (bi, 0, ci), pipeline_mode=pl.Buffered(1))
    args = [a] + ([b] if b is not None else [])
    specs = [sig] * len(args)
    args += [kf, d, tf1, gf, gi, ti2]
    specs += [pl.BlockSpec((1, h, 2 * DFT_MINOR, cb), lambda ci, bi: (order, 0, 0, ci),
                           pipeline_mode=pl.Buffered(1)),
              pl.BlockSpec((1, 1, cb), lambda ci, bi: (order, 0, ci)),
              _const_spec(tf1.shape), _const_spec(gf.shape), _const_spec(gi.shape), _const_spec(ti2.shape)]
    return pl.pallas_call(
        functools.partial(_longconv_body, gated=b is not None, l=l),
        grid=(c // cb, bn),
        in_specs=specs,
        out_specs=sig,
        out_shape=jax.ShapeDtypeStruct((bn, l, c), F32),
        scratch_shapes=[pltpu.VMEM((DFT_MINOR * 2 * hp, cb), F32), pltpu.VMEM((hp * 2 * DFT_MINOR, cb), F32)],
        compiler_params=_cparams(("parallel", "parallel")),
        name="hyena_longconv",
    )(*args)


@functools.lru_cache(maxsize=None)
def _dft_tables(l):
    n = 2 * l
    n1, nh, h, hp = _dft_dims(l)
    m = DFT_MINOR
    n2i = np.arange(m)[:, None, None]
    k1i = np.arange(h)[None, :, None]
    n1i = np.arange(nh)[None, None, :]
    ang = 2.0 * np.pi * ((k1i * (m * n1i + n2i)) % n) / n
    tf1 = np.zeros((m, 2 * hp, nh), np.float32)
    tf1[:, :h] = np.cos(ang)
    tf1[:, hp:hp + h] = -np.sin(ang)
    w = np.full((h,), 2.0)
    w[0] = 1.0
    w[-1] = 1.0
    ti2 = np.zeros((m, nh, 2 * hp), np.float32)
    ti2[:, :, :h] = np.transpose(np.cos(ang) * (w[None, :, None] / n), (0, 2, 1))
    ti2[:, :, hp:hp + h] = np.transpose(-np.sin(ang) * (w[None, :, None] / n), (0, 2, 1))
    th = 2.0 * np.pi * ((np.arange(m)[:, None] * np.arange(m)[None, :]) % m) / m
    cm, sm = np.cos(th), np.sin(th)
    gf = np.block([[cm, sm], [-sm, cm]]).astype(np.float32)
    gi = np.block([[cm, -sm], [sm, cm]]).astype(np.float32)
    return tf1, gf, gi, ti2


@functools.lru_cache(maxsize=None)
def _filter_tables(l):
    bands = (HY_EMB - 1) // 2
    t = np.linspace(0.0, 1.0, l)[:, None]
    ang = 2.0 * np.pi * np.arange(l)[:, None] / l
    fb = np.linspace(1e-4, bands - 1, bands)[None, :]
    z = np.concatenate([t, np.cos(fb * ang), -np.sin(fb * ang)], axis=-1)
    idx = np.concatenate([np.arange(l), [0], np.arange(l - 1, 0, -1)])
    zt = np.zeros((2 * l, LANE), np.float32)
    zt[:, :HY_EMB] = z[idx]
    zt[:, HY_EMB] = 1.0
    zt[l, HY_EMB] = 0.0
    max_decay = math.log(HY_TARGET) / HY_FAST
    min_decay = math.log(HY_TARGET) / HY_SLOW
    deltas = np.abs(np.linspace(min_decay, max_decay, HY_WIDTH))[None, :].astype(np.float32)
    return zt, deltas


def _rope_rows():
    inv_freq = 1.0 / (ROPE_THETA ** (np.arange(0, ROPE, 2, dtype=np.float32) / ROPE))
    half = ROPE // 2
    freq = np.zeros((1, HEAD_PAD), np.float32)
    sign = np.zeros((1, HEAD_PAD), np.float32)
    freq[0, NOPE:NOPE + half] = inv_freq
    freq[0, NOPE + half:NOPE + ROPE] = inv_freq
    sign[0, NOPE:NOPE + half] = -1.0
    sign[0, NOPE + half:NOPE + ROPE] = 1.0
    vone = np.zeros((1, HEAD_PAD), np.float32)
    vone[0, VDIM] = 1.0
    return freq, sign, vone


def _pad_cols(w, width, at=0):
    out = jnp.zeros(w.shape[:-1] + (width,), w.dtype)
    return out.at[..., at:at + w.shape[-1]].set(w)


def _swap_halves(w):
    half = w.shape[-1] // 2
    return jnp.concatenate([w[..., half:], w[..., :half]], axis=-1)


def _pad2(w, rows, cols):
    out = jnp.zeros((rows, cols), w.dtype)
    return out.at[:w.shape[0], :w.shape[1]].set(w)


def kernel(x, positions, ffn1_norm, ffn1_w_gate, ffn1_w_up, ffn1_w_down, mix_norm, w_in, mla_q_norm, mla_w_qb, mla_kv_norm, mla_w_kvb, conv_dw_w, conv_dw_b, conv_ln_g, conv_ln_b, hy_short_w, hy_short_b, hy_filt_w1, hy_filt_b1, hy_filt_w2, hy_filt_b2, hy_filt_w3, hy_filt_b3, hy_filt_w4, hy_filt_freq, hy_bias_d, out_norm, w_out, ffn2_norm, ffn2_w_gate, ffn2_w_up, ffn2_w_down, final_norm):
    b, l, d = x.shape
    n = b * l
    depth = w_in.shape[0]
    row = lambda v: v.reshape(1, -1).astype(F32)

    tf1, gf, gi, ti2 = (jnp.asarray(t).astype(BF16) for t in _dft_tables(l))
    zt_np, deltas_np = _filter_tables(l)
    zt, deltas = jnp.asarray(zt_np), jnp.asarray(deltas_np)
    freq, sign, vone = (jnp.asarray(t) for t in _rope_rows())
    pos = positions.astype(F32).reshape(n, 1)

    xs = x.reshape(n, d)
    for i in range(depth):
        xs = _ffn_call(xs, row(ffn1_norm[i]), ffn1_w_gate[i].astype(BF16), ffn1_w_up[i].astype(BF16),
                       ffn1_w_down[i].astype(BF16))

        wi = w_in[i]
        kpe_w = wi[:, OFF_KPE:OFF_CONV]
        win = jnp.concatenate([
            wi[:, OFF_Q:OFF_KV], wi[:, OFF_KV:OFF_KPE],
            _pad_cols(kpe_w, HEAD_PAD, NOPE), _pad_cols(_swap_halves(kpe_w), HEAD_PAD, NOPE),
            wi[:, OFF_CONV:]], axis=1).astype(BF16)
        wq = mla_w_qb[i].reshape(Q_RANK, HEADS, QK_DIM)
        wqa = _pad_cols(wq, HEAD_PAD).reshape(Q_RANK, HEADS * HEAD_PAD).astype(BF16)
        wq_sw = jnp.concatenate([jnp.zeros_like(wq[..., :NOPE]), _swap_halves(wq[..., NOPE:])], axis=-1)
        wqb = _pad_cols(wq_sw, HEAD_PAD).reshape(Q_RANK, HEADS * HEAD_PAD).astype(BF16)
        wkv = mla_w_kvb[i].reshape(KV_RANK, HEADS, NOPE + VDIM)
        wk = _pad_cols(wkv[..., :NOPE], HEAD_PAD).reshape(KV_RANK, HEADS * HEAD_PAD).astype(BF16)
        wv = _pad_cols(wkv[..., NOPE:], HEAD_PAD).reshape(KV_RANK, HEADS * HEAD_PAD).astype(BF16)

        q, k, v, loc = _proj_call(xs, pos, b, l, row(mix_norm[i]), win, row(mla_q_norm[i]), wqa, wqb,
                                  row(mla_kv_norm[i]), wk, wv, freq, sign, vone)
        y_mla = _attn_call(q, k, v).reshape(n, MLA_WIDTH)

        y_conv, hv, hx1, hx2 = _local_call(
            loc.reshape(b, l, LOC_COLS), conv_dw_w[i], row(conv_dw_b[i]), row(conv_ln_g[i]),
            row(conv_ln_b[i]), hy_short_w[i], row(hy_short_b[i]))

        w4 = hy_filt_w4[i].reshape(HY_HID, HY_ORDER, 2, HY_WIDTH)
        w4s = jnp.stack([_pad2(w4[:, :, s].reshape(HY_HID, HY_ORDER * HY_WIDTH), LANE, HY_ORDER * HY_WIDTH)
                         for s in range(2)])
        hrow = lambda v_: _pad2(v_.reshape(1, -1), 1, LANE)
        kext, ssum = _filter_call(
            zt, _pad2(hy_filt_w1[i], LANE, LANE), hrow(hy_filt_b1[i]), _pad2(hy_filt_w2[i], LANE, LANE),
            hrow(hy_filt_b2[i]), _pad2(hy_filt_w3[i], LANE, LANE), hrow(hy_filt_b3[i]), w4s,
            hrow(hy_filt_freq[i]), deltas, l)
        kf = _spectrum_call(kext, ssum, tf1, gf, l)
        dd = hy_bias_d[i].reshape(HY_ORDER, 1, HY_WIDTH)
        y1 = _longconv_call(hv, None, kf, dd, tf1, gf, gi, ti2, 0)
        y2 = _longconv_call(hx1, y1, kf, dd, tf1, gf, gi, ti2, 1)

        last = i == depth - 1
        xs = _ffn_call(xs, row(ffn2_norm[i]), ffn2_w_gate[i].astype(BF16), ffn2_w_up[i].astype(BF16),
                       ffn2_w_down[i].astype(BF16),
                       mix=(y_mla, y_conv.reshape(n, -1), y2.reshape(n, -1), hx2.reshape(n, -1),
                            row(out_norm[i]), w_out[i].astype(BF16)),
                       final_g=row(final_norm) if last else None)
    return xs.reshape(b, l, d)
```

```python
import functools
import math

import numpy as np
import jax
import jax.numpy as jnp
from jax import lax
from jax.experimental import pallas as pl
from jax.experimental.pallas import tpu as pltpu

F32 = jnp.float32
BF16 = jnp.bfloat16

D_MODEL = 1024
NORM_EPS = 1e-6
D_FF = 2816
HEADS = 8
NOPE = 64
ROPE = 32
VDIM = 64
QK_DIM = NOPE + ROPE
Q_RANK = 256
KV_RANK = 128
MLA_WIDTH = HEADS * VDIM
ROPE_THETA = 10000.0
CONV_WIDTH = 256
CONV_KERNEL = 31
HY_WIDTH = 256
HY_ORDER = 2
HY_SHORT = 3
HY_EMB = 33
HY_HID = 64
HY_FAST, HY_SLOW, HY_TARGET = 0.3, 1.5, 1e-2
OFF_Q = 0
OFF_KV = OFF_Q + Q_RANK
OFF_KPE = OFF_KV + KV_RANK
OFF_CONV = OFF_KPE + ROPE
OFF_HY = OFF_CONV + 2 * CONV_WIDTH
IN_COLS = OFF_HY + 3 * HY_WIDTH
LOC_COLS = 2 * CONV_WIDTH + 3 * HY_WIDTH

LANE = 128
HEAD_PAD = LANE
DFT_MINOR = 128
HALO = 16
VMEM_LIMIT = 56 * 1024 * 1024


def _cparams(sem):
    return pltpu.CompilerParams(dimension_semantics=sem, vmem_limit_bytes=VMEM_LIMIT)


def _const_spec(shape):
    nd = len(shape)
    return pl.BlockSpec(shape, lambda *_: (0,) * nd, pipeline_mode=pl.Buffered(1))


def _rms(x, g):
    ms = jnp.mean(x * x, axis=-1, keepdims=True)
    return x * lax.rsqrt(ms + NORM_EPS) * g


def _dot(a, b):
    return jnp.dot(a, b, preferred_element_type=F32)


def _ffn_body(*refs, mix, final, ff_chunk):
    refs = list(refs)
    x_ref = refs.pop(0)
    if mix:
        ym_ref, yc_ref, yh_ref, x2_ref, on_ref, wo_ref = refs[:6]
        refs = refs[6:]
    g_ref, wg_ref, wu_ref, wd_ref = refs[:4]
    refs = refs[4:]
    if final:
        fin_ref = refs.pop(0)
    o_ref, a_scr = refs

    x = x_ref[...]
    if mix:
        on = on_ref[...]
        e1, e2 = MLA_WIDTH, MLA_WIDTH + CONV_WIDTH
        y = jnp.concatenate([
            _rms(ym_ref[...], on[:, :e1]),
            _rms(yc_ref[...], on[:, e1:e2]),
            _rms(yh_ref[...] * x2_ref[...], on[:, e2:]),
        ], axis=-1).astype(BF16)
        x = x + _dot(y, wo_ref[...])
    h = _rms(x, g_ref[...]).astype(BF16)
    for c in range(D_FF // ff_chunk):
        sl = slice(c * ff_chunk, (c + 1) * ff_chunk)
        g = _dot(h, wg_ref[:, sl])
        u = _dot(h, wu_ref[:, sl])
        a_scr[:, sl] = (g * jax.nn.sigmoid(g) * u).astype(BF16)
    y = x + 0.5 * _dot(a_scr[...], wd_ref[...])
    if final:
        y = _rms(y, fin_ref[...])
    o_ref[...] = y


def _ffn_call(x, norm_g, wg, wu, wd, mix=None, final_g=None, tm=512, ff_chunk=256):
    n, d = x.shape
    tm = min(tm, n)
    row = lambda w: pl.BlockSpec((tm, w), lambda i: (i, 0))
    args, specs = [x], [row(d)]
    if mix is not None:
        ym, yc, yh, x2, on, wo = mix
        args += [ym, yc, yh, x2, on, wo]
        specs += [row(ym.shape[1]), row(yc.shape[1]), row(yh.shape[1]), row(x2.shape[1]),
                  _const_spec(on.shape), _const_spec(wo.shape)]
    args += [norm_g, wg, wu, wd]
    specs += [_const_spec(norm_g.shape), _const_spec(wg.shape), _const_spec(wu.shape), _const_spec(wd.shape)]
    if final_g is not None:
        args.append(final_g)
        specs.append(_const_spec(final_g.shape))
    return pl.pallas_call(
        functools.partial(_ffn_body, mix=mix is not None, final=final_g is not None, ff_chunk=ff_chunk),
        grid=(n // tm,),
        in_specs=specs,
        out_specs=row(d),
        out_shape=jax.ShapeDtypeStruct((n, d), F32),
        scratch_shapes=[pltpu.VMEM((tm, D_FF), BF16)],
        compiler_params=_cparams(("parallel",)),
        name="ffn_mix" if mix is not None else "ffn",
    )(*args)


def _proj_body(x_ref, pos_ref, g_ref, win_ref, qn_ref, wqa_ref, wqb_ref, kvn_ref, wk_ref, wv_ref,
               freq_ref, sign_ref, vone_ref, q_ref, kt_ref, v_ref, loc_ref):
    xn = _rms(x_ref[...], g_ref[...]).astype(BF16)
    h = _dot(xn, win_ref[...])
    ang = pos_ref[...] * freq_ref[...]
    cos = jnp.cos(ang)
    sin = jnp.sin(ang) * sign_ref[...]
    o1 = Q_RANK
    o2 = o1 + KV_RANK
    o3 = o2 + HEAD_PAD
    o4 = o3 + HEAD_PAD
    cq = _rms(h[:, :o1], qn_ref[...]).astype(BF16)
    ckv = _rms(h[:, o1:o2], kvn_ref[...]).astype(BF16)
    kpe = h[:, o2:o3] * cos + h[:, o3:o4] * sin
    loc_ref[...] = h[:, o4:]
    qa = _dot(cq, wqa_ref[...])
    qb = _dot(cq, wqb_ref[...])
    kk = _dot(ckv, wk_ref[...])
    vv = _dot(ckv, wv_ref[...])
    scale = QK_DIM ** -0.5 * math.log2(math.e)
    cs, ss = cos * scale, sin * scale
    vone = vone_ref[...]
    for hd in range(HEADS):
        sl = slice(hd * HEAD_PAD, (hd + 1) * HEAD_PAD)
        q_ref[0, hd] = (qa[:, sl] * cs + qb[:, sl] * ss).astype(BF16)
        kt_ref[0, hd] = (kk[:, sl] + kpe).T.astype(BF16)
        v_ref[0, hd] = (vv[:, sl] + vone).astype(BF16)


def _proj_call(x, pos, b, l, g, win, qn, wqa, wqb, kvn, wk, wv, freq, sign, vone, tm=512):
    n, d = x.shape
    tm = min(tm, l)
    nlt = l // tm
    head_spec = pl.BlockSpec((1, HEADS, tm, HEAD_PAD), lambda i: (i // nlt, 0, i % nlt, 0))
    head_shape = jax.ShapeDtypeStruct((b, HEADS, l, HEAD_PAD), BF16)
    kt_spec = pl.BlockSpec((1, HEADS, HEAD_PAD, tm), lambda i: (i // nlt, 0, 0, i % nlt))
    kt_shape = jax.ShapeDtypeStruct((b, HEADS, HEAD_PAD, l), BF16)
    consts = [g, win, qn, wqa, wqb, kvn, wk, wv, freq, sign, vone]
    return pl.pallas_call(
        _proj_body,
        grid=(n // tm,),
        in_specs=[pl.BlockSpec((tm, d), lambda i: (i, 0)), pl.BlockSpec((tm, 1), lambda i: (i, 0))]
        + [_const_spec(c.shape) for c in consts],
        out_specs=[head_spec, kt_spec, head_spec, pl.BlockSpec((tm, LOC_COLS), lambda i: (i, 0))],
        out_shape=[head_shape, kt_shape, head_shape, jax.ShapeDtypeStruct((n, LOC_COLS), F32)],
        compiler_params=_cparams(("parallel",)),
        name="in_proj",
    )(x, pos, *consts)


def _attn_body(q_ref, kt_ref, v_ref, o_ref, *, tk, heads_per_step):
    l = kt_ref.shape[3]
    tq = q_ref.shape[2]
    hps = heads_per_step
    qs = [q_ref[0, hd] for hd in range(hps)]

    def step(j, carry):
        off = pl.multiple_of(j * tk, tk)
        new = []
        for hd in range(hps):
            m, acc = carry[hd]
            kt = kt_ref[0, hd, :, pl.ds(off, tk)]
            v = v_ref[0, hd, pl.ds(off, tk), :]
            s = _dot(qs[hd], kt)
            m_new = jnp.maximum(m, jnp.max(s, axis=-1, keepdims=True))
            alpha = jnp.exp2(m - m_new)
            p = jnp.exp2(s - m_new).astype(BF16)
            new.append((m_new, alpha * acc + _dot(p, v)))
        return tuple(new)

    init = tuple((jnp.full((tq, 1), -1e30, F32), jnp.zeros((tq, HEAD_PAD), F32)) for _ in range(hps))
    fin = lax.fori_loop(0, l // tk, step, init)
    o_ref[0] = jnp.concatenate([acc[:, :VDIM] / acc[:, VDIM:VDIM + 1] for _, acc in fin], axis=-1)


def _attn_call(q, kt, v, tq=512, tk=2048, heads_per_step=2):
    b, hn, l, dp = q.shape
    tq, tk = min(tq, l), min(tk, l)
    hps = heads_per_step
    return pl.pallas_call(
        functools.partial(_attn_body, tk=tk, heads_per_step=hps),
        grid=(b, hn // hps, l // tq),
        in_specs=[pl.BlockSpec((1, hps, tq, dp), lambda bi, hi, qi: (bi, hi, qi, 0)),
                  pl.BlockSpec((1, hps, dp, l), lambda bi, hi, qi: (bi, hi, 0, 0)),
                  pl.BlockSpec((1, hps, l, dp), lambda bi, hi, qi: (bi, hi, 0, 0))],
        out_specs=pl.BlockSpec((1, tq, hps * VDIM), lambda bi, hi, qi: (bi, qi, hi)),
        out_shape=jax.ShapeDtypeStruct((b, l, hn * VDIM), F32),
        compiler_params=_cparams(("parallel", "parallel", "arbitrary")),
        name="mla_attn",
    )(q, kt, v)


def _local_body(cur_ref, prev_ref, next_ref, cw_ref, cb_ref, lg_ref, lb_ref, sw_ref, sb_ref,
                yc_ref, hv_ref, hx1_ref, hx2_ref, u_scr, s_scr):
    t = cur_ref.shape[1]
    i = pl.program_id(1)
    first = i == 0
    last = i == pl.num_programs(1) - 1
    cw = 2 * CONV_WIDTH

    def glu(w):
        return w[:, :CONV_WIDTH] * jax.nn.sigmoid(w[:, CONV_WIDTH:cw])

    prev = prev_ref[0]
    nxt = next_ref[0]
    cur = cur_ref[0]
    pmask = jnp.where(first, 0.0, 1.0)
    nmask = jnp.where(last, 0.0, 1.0)
    u_scr[0:HALO, :] = glu(prev) * pmask
    u_scr[HALO:HALO + t, :] = glu(cur)
    u_scr[HALO + t:, :] = glu(nxt) * nmask
    s_scr[0:HALO, :] = prev[:, cw:] * pmask
    s_scr[HALO:HALO + t, :] = cur[:, cw:]
    s_scr[HALO + t:, :] = nxt[:, cw:] * nmask

    pad = CONV_KERNEL // 2
    acc = jnp.zeros((t, CONV_WIDTH), F32) + cb_ref[...]
    for kk in range(CONV_KERNEL):
        acc = acc + u_scr[pl.ds(HALO - pad + kk, t), :] * cw_ref[kk:kk + 1, :]
    mu = jnp.mean(acc, axis=-1, keepdims=True)
    cen = acc - mu
    var = jnp.mean(cen * cen, axis=-1, keepdims=True)
    yn = cen * lax.rsqrt(var + NORM_EPS) * lg_ref[...] + lb_ref[...]
    yc_ref[0] = yn * jax.nn.sigmoid(yn)

    sp = HY_SHORT // 2
    hs = jnp.zeros((t, 3 * HY_WIDTH), F32) + sb_ref[...]
    for kk in range(HY_SHORT):
        hs = hs + s_scr[pl.ds(HALO - sp + kk, t), :] * sw_ref[kk:kk + 1, :]
    hv_ref[0] = hs[:, :HY_WIDTH]
    hx1_ref[0] = hs[:, HY_WIDTH:2 * HY_WIDTH]
    hx2_ref[0] = hs[:, 2 * HY_WIDTH:]


def _local_call(loc, cw, cb, lg, lb, sw, sb, t=512):
    b, l, c = loc.shape
    t = min(t, l)
    r = t // HALO
    nh = l // HALO
    consts = [cw, cb, lg, lb, sw, sb]
    out_spec = pl.BlockSpec((1, t, HY_WIDTH), lambda bi, i: (bi, i, 0))
    out_shape = jax.ShapeDtypeStruct((b, l, HY_WIDTH), F32)
    return pl.pallas_call(
        _local_body,
        grid=(b, l // t),
        in_specs=[pl.BlockSpec((1, t, c), lambda bi, i: (bi, i, 0)),
                  pl.BlockSpec((1, HALO, c), lambda bi, i: (bi, jnp.maximum(i * r - 1, 0), 0)),
                  pl.BlockSpec((1, HALO, c), lambda bi, i: (bi, jnp.minimum((i + 1) * r, nh - 1), 0))]
        + [_const_spec(a.shape) for a in consts],
        out_specs=[out_spec] * 4,
        out_shape=[out_shape] * 4,
        scratch_shapes=[pltpu.VMEM((t + 2 * HALO, CONV_WIDTH), F32),
                        pltpu.VMEM((t + 2 * HALO, 3 * HY_WIDTH), F32)],
        compiler_params=_cparams(("parallel", "parallel")),
        name="local_mixers",
    )(loc, loc, loc, *consts)


def _filter_body(z_ref, w1_ref, b1_ref, w2_ref, b2_ref, w3_ref, b3_ref, w4_ref, fr_ref, dl_ref,
                 k_ref, s_ref):
    hp = lax.Precision.HIGHEST
    z = z_ref[...]
    fr = fr_ref[...]
    hd = jnp.sin(fr * (jnp.dot(z, w1_ref[...], precision=hp, preferred_element_type=F32) + b1_ref[...]))
    hd = jnp.sin(fr * (jnp.dot(hd, w2_ref[...], precision=hp, preferred_element_type=F32) + b2_ref[...]))
    hd = jnp.sin(fr * (jnp.dot(hd, w3_ref[...], precision=hp, preferred_element_type=F32) + b3_ref[...]))
    h = jnp.dot(hd, w4_ref[0], precision=hp, preferred_element_type=F32)
    tcol = z[:, 0:1]
    mask = z[:, HY_EMB:HY_EMB + 1]
    win = jnp.exp(-tcol * dl_ref[...]) * mask

    @pl.when(pl.program_id(0) == 0)
    def _():
        s_ref[...] = jnp.zeros_like(s_ref)

    for o in range(HY_ORDER):
        ko = h[:, o * HY_WIDTH:(o + 1) * HY_WIDTH] * win
        k_ref[o] = ko
        s_ref[o] += jnp.broadcast_to(jnp.sum(jnp.abs(ko), axis=0, keepdims=True), s_ref.shape[1:])


def _filter_call(zt, w1, b1, w2, b2, w3, b3, w4, fr, dl, l, t=512):
    n2 = zt.shape[0]
    t = min(t, l)
    per_half = l // t
    consts_a = [w1, b1, w2, b2, w3, b3]
    return pl.pallas_call(
        _filter_body,
        grid=(n2 // t,),
        in_specs=[pl.BlockSpec((t, zt.shape[1]), lambda i: (i, 0))]
        + [_const_spec(a.shape) for a in consts_a]
        + [pl.BlockSpec((1,) + w4.shape[1:], lambda i: (i // per_half, 0, 0)),
           _const_spec(fr.shape), _const_spec(dl.shape)],
        out_specs=[pl.BlockSpec((HY_ORDER, t, HY_WIDTH), lambda i: (0, i, 0)),
                   pl.BlockSpec((HY_ORDER, 8, HY_WIDTH), lambda i: (0, 0, 0))],
        out_shape=[jax.ShapeDtypeStruct((HY_ORDER, n2, HY_WIDTH), F32),
                   jax.ShapeDtypeStruct((HY_ORDER, 8, HY_WIDTH), F32)],
        compiler_params=_cparams(("arbitrary",)),
        name="hyena_filter",
    )(zt, *consts_a, w4, fr, dl)


def _dft_dims(l):
    n1 = 2 * l // DFT_MINOR
    nh = n1 // 2
    h = nh + 1
    hp = -(-h // 8) * 8
    return n1, nh, h, hp


def _fwd_stage1(load_slab, tf1_ref, x1_scr, hp):
    def body(n2, c):
        u = load_slab(n2).astype(BF16)
        x1 = _dot(tf1_ref[n2], u)
        x1_scr[pl.ds(pl.multiple_of(n2 * 2 * hp, 8), 2 * hp), :] = x1
        return c
    lax.fori_loop(0, DFT_MINOR, body, 0)


def _fwd_stage2(k1, gf_ref, x1_scr, hp):
    are = x1_scr[pl.ds(k1, DFT_MINOR, stride=2 * hp), :]
    aim = x1_scr[pl.ds(hp + k1, DFT_MINOR, stride=2 * hp), :]
    rhs = jnp.concatenate([are, aim], axis=0).astype(BF16)
    return _dot(gf_ref[...], rhs)


def _spectrum_body(kf_ref, kb_ref, s_ref, tf1_ref, gf_ref, o_ref, x1_scr, *, l):
    _, nh, h, hp = _dft_dims(l)
    inv = 1.0 / s_ref[0, 0:1, :]

    for half, ref in enumerate((kf_ref, kb_ref)):
        _fwd_stage1(lambda n2, ref=ref: ref[0, pl.ds(n2, nh, stride=DFT_MINOR), :], tf1_ref, x1_scr, hp)

        def body(k1, c, half=half):
            x = _fwd_stage2(k1, gf_ref, x1_scr, hp) * inv
            if half == 0:
                o_ref[0, k1] = x
            else:
                sgn = (1 - 2 * (k1 % 2)).astype(F32)
                o_ref[0, k1] = o_ref[0, k1] + sgn * x
            return c
        lax.fori_loop(0, h, body, 0)


def _spectrum_call(kext, ssum, tf1, gf, l):
    _, nh, h, hp = _dft_dims(l)
    order, _, c = kext.shape
    cb = LANE
    return pl.pallas_call(
        functools.partial(_spectrum_body, l=l),
        grid=(order, c // cb),
        in_specs=[pl.BlockSpec((1, l, cb), lambda o, ci: (o, 0, ci), pipeline_mode=pl.Buffered(1)),
                  pl.BlockSpec((1, l, cb), lambda o, ci: (o, 1, ci), pipeline_mode=pl.Buffered(1)),
                  pl.BlockSpec((1, 8, cb), lambda o, ci: (o, 0, ci)),
                  _const_spec(tf1.shape), _const_spec(gf.shape)],
        out_specs=pl.BlockSpec((1, h, 2 * DFT_MINOR, cb), lambda o, ci: (o, 0, 0, ci),
                               pipeline_mode=pl.Buffered(1)),
        out_shape=jax.ShapeDtypeStruct((order, h, 2 * DFT_MINOR, c), F32),
        scratch_shapes=[pltpu.VMEM((DFT_MINOR * 2 * hp, cb), F32)],
        compiler_params=_cparams(("parallel", "parallel")),
        name="hyena_spectrum",
    )(kext, kext, ssum, tf1, gf)


def _longconv_body(*refs, gated, l):
    if gated:
        a_ref, b_ref, kf_ref, d_ref, tf1_ref, gf_ref, gi_ref, ti2_ref, o_ref, x1_scr, z_scr = refs
    else:
        a_ref, kf_ref, d_ref, tf1_ref, gf_ref, gi_ref, ti2_ref, o_ref, x1_scr, z_scr = refs
        b_ref = None
    _, nh, h, hp = _dft_dims(l)
    m = DFT_MINOR

    def load_slab(n2):
        u = a_ref[0, pl.ds(n2, nh, stride=m), :]
        if gated:
            u = u * b_ref[0, pl.ds(n2, nh, stride=m), :]
        return u

    _fwd_stage1(load_slab, tf1_ref, x1_scr, hp)

    def mid(k1, c):
        x = _fwd_stage2(k1, gf_ref, x1_scr, hp)
        xre, xim = x[:m], x[m:]
        kre = kf_ref[0, k1, 0:m, :]
        kim = kf_ref[0, k1, m:2 * m, :]
        y = jnp.concatenate([xre * kre - xim * kim, xre * kim + xim * kre], axis=0).astype(BF16)
        z_scr[pl.ds(pl.multiple_of(k1 * 2 * m, 2 * m), 2 * m), :] = _dot(gi_ref[...], y)
        return c
    lax.fori_loop(0, h, mid, 0)
    if hp > h:
        z_scr[h * 2 * m:, :] = jnp.zeros(((hp - h) * 2 * m, z_scr.shape[1]), F32)

    def last(n2, c):
        zre = z_scr[pl.ds(n2, hp, stride=2 * m), :]
        zim = z_scr[pl.ds(m + n2, hp, stride=2 * m), :]
        rhs = jnp.concatenate([zre, zim], axis=0).astype(BF16)
        y = _dot(ti2_ref[n2], rhs)
        o_ref[0, pl.ds(n2, nh, stride=m), :] = y
        return c
    lax.fori_loop(0, m, last, 0)

    rows = min(512, l)

    def epi(i, c):
        sl = pl.ds(pl.multiple_of(i * rows, rows), rows)
        u = a_ref[0, sl, :]
        if gated:
            u = u * b_ref[0, sl, :]
        o_ref[0, sl, :] = o_ref[0, sl, :] + u * d_ref[0]
        return c
    lax.fori_loop(0, l // rows, epi, 0)


def _longconv_call(a, b, kf, d, tf1, gf, gi, ti2, order):
    bn, l, c = a.shape
    _, nh, h, hp = _dft_dims(l)
    cb = LANE
    sig = pl.BlockSpec((1, l, cb), lambda ci, bi: (bi, 0, ci), pipeline_mode=pl.Buffered(1))
    args = [a] + ([b] if b is not None else [])
    specs = [sig] * len(args)
    args += [kf, d, tf1, gf, gi, ti2]
    specs += [pl.BlockSpec((1, h, 2 * DFT_MINOR, cb), lambda ci, bi: (order, 0, 0, ci),
                           pipeline_mode=pl.Buffered(1)),
              pl.BlockSpec((1, 1, cb), lambda ci, bi: (order, 0, ci)),
              _const_spec(tf1.shape), _const_spec(gf.shape), _const_spec(gi.shape), _const_spec(ti2.shape)]
    return pl.pallas_call(
        functools.partial(_longconv_body, gated=b is not None, l=l),
        grid=(c // cb, bn),
        in_specs=specs,
        out_specs=sig,
        out_shape=jax.ShapeDtypeStruct((bn, l, c), F32),
        scratch_shapes=[pltpu.VMEM((DFT_MINOR * 2 * hp, cb), F32), pltpu.VMEM((hp * 2 * DFT_MINOR, cb), F32)],
        compiler_params=_cparams(("parallel", "parallel")),
        name="hyena_longconv",
    )(*args)


@functools.lru_cache(maxsize=None)
def _dft_tables(l):
    n = 2 * l
    n1, nh, h, hp = _dft_dims(l)
    m = DFT_MINOR
    n2i = np.arange(m)[:, None, None]
    k1i = np.arange(h)[None, :, None]
    n1i = np.arange(nh)[None, None, :]
    ang = 2.0 * np.pi * ((k1i * (m * n1i + n2i)) % n) / n
    tf1 = np.zeros((m, 2 * hp, nh), np.float32)
    tf1[:, :h] = np.cos(ang)
    tf1[:, hp:hp + h] = -np.sin(ang)
    w = np.full((h,), 2.0)
    w[0] = 1.0
    w[-1] = 1.0
    ti2 = np.zeros((m, nh, 2 * hp), np.float32)
    ti2[:, :, :h] = np.transpose(np.cos(ang) * (w[None, :, None] / n), (0, 2, 1))
    ti2[:, :, hp:hp + h] = np.transpose(-np.sin(ang) * (w[None, :, None] / n), (0, 2, 1))
    th = 2.0 * np.pi * ((np.arange(m)[:, None] * np.arange(m)[None, :]) % m) / m
    cm, sm = np.cos(th), np.sin(th)
    gf = np.block([[cm, sm], [-sm, cm]]).astype(np.float32)
    gi = np.block([[cm, -sm], [sm, cm]]).astype(np.float32)
    return tf1, gf, gi, ti2


@functools.lru_cache(maxsize=None)
def _filter_tables(l):
    bands = (HY_EMB - 1) // 2
    t = np.linspace(0.0, 1.0, l)[:, None]
    ang = 2.0 * np.pi * np.arange(l)[:, None] / l
    fb = np.linspace(1e-4, bands - 1, bands)[None, :]
    z = np.concatenate([t, np.cos(fb * ang), -np.sin(fb * ang)], axis=-1)
    idx = np.concatenate([np.arange(l), [0], np.arange(l - 1, 0, -1)])
    zt = np.zeros((2 * l, LANE), np.float32)
    zt[:, :HY_EMB] = z[idx]
    zt[:, HY_EMB] = 1.0
    zt[l, HY_EMB] = 0.0
    max_decay = math.log(HY_TARGET) / HY_FAST
    min_decay = math.log(HY_TARGET) / HY_SLOW
    deltas = np.abs(np.linspace(min_decay, max_decay, HY_WIDTH))[None, :].astype(np.float32)
    return zt, deltas


def _rope_rows():
    inv_freq = 1.0 / (ROPE_THETA ** (np.arange(0, ROPE, 2, dtype=np.float32) / ROPE))
    half = ROPE // 2
    freq = np.zeros((1, HEAD_PAD), np.float32)
    sign = np.zeros((1, HEAD_PAD), np.float32)
    freq[0, NOPE:NOPE + half] = inv_freq
    freq[0, NOPE + half:NOPE + ROPE] = inv_freq
    sign[0, NOPE:NOPE + half] = -1.0
    sign[0, NOPE + half:NOPE + ROPE] = 1.0
    vone = np.zeros((1, HEAD_PAD), np.float32)
    vone[0, VDIM] = 1.0
    return freq, sign, vone


def _pad_cols(w, width, at=0):
    out = jnp.zeros(w.shape[:-1] + (width,), w.dtype)
    return out.at[..., at:at + w.shape[-1]].set(w)


def _swap_halves(w):
    half = w.shape[-1] // 2
    return jnp.concatenate([w[..., half:], w[..., :half]], axis=-1)


def _pad2(w, rows, cols):
    out = jnp.zeros((rows, cols), w.dtype)
    return out.at[:w.shape[0], :w.shape[1]].set(w)


def kernel(x, positions, ffn1_norm, ffn1_w_gate, ffn1_w_up, ffn1_w_down, mix_norm, w_in, mla_q_norm, mla_w_qb, mla_kv_norm, mla_w_kvb, conv_dw_w, conv_dw_b, conv_ln_g, conv_ln_b, hy_short_w, hy_short_b, hy_filt_w1, hy_filt_b1, hy_filt_w2, hy_filt_b2, hy_filt_w3, hy_filt_b3, hy_filt_w4, hy_filt_freq, hy_bias_d, out_norm, w_out, ffn2_norm, ffn2_w_gate, ffn2_w_up, ffn2_w_down, final_norm):
    b, l, d = x.shape
    n = b * l
    depth = w_in.shape[0]
    row = lambda v: v.reshape(1, -1).astype(F32)

    tf1, gf, gi, ti2 = (jnp.asarray(t).astype(BF16) for t in _dft_tables(l))
    zt_np, deltas_np = _filter_tables(l)
    zt, deltas = jnp.asarray(zt_np), jnp.asarray(deltas_np)
    freq, sign, vone = (jnp.asarray(t) for t in _rope_rows())
    pos = positions.astype(F32).reshape(n, 1)

    xs = x.reshape(n, d)
    for i in range(depth):
        xs = _ffn_call(xs, row(ffn1_norm[i]), ffn1_w_gate[i].astype(BF16), ffn1_w_up[i].astype(BF16),
                       ffn1_w_down[i].astype(BF16))

        wi = w_in[i]
        kpe_w = wi[:, OFF_KPE:OFF_CONV]
        win = jnp.concatenate([
            wi[:, OFF_Q:OFF_KV], wi[:, OFF_KV:OFF_KPE],
            _pad_cols(kpe_w, HEAD_PAD, NOPE), _pad_cols(_swap_halves(kpe_w), HEAD_PAD, NOPE),
            wi[:, OFF_CONV:]], axis=1).astype(BF16)
        wq = mla_w_qb[i].reshape(Q_RANK, HEADS, QK_DIM)
        wqa = _pad_cols(wq, HEAD_PAD).reshape(Q_RANK, HEADS * HEAD_PAD).astype(BF16)
        wq_sw = jnp.concatenate([jnp.zeros_like(wq[..., :NOPE]), _swap_halves(wq[..., NOPE:])], axis=-1)
        wqb = _pad_cols(wq_sw, HEAD_PAD).reshape(Q_RANK, HEADS * HEAD_PAD).astype(BF16)
        wkv = mla_w_kvb[i].reshape(KV_RANK, HEADS, NOPE + VDIM)
        wk = _pad_cols(wkv[..., :NOPE], HEAD_PAD).reshape(KV_RANK, HEADS * HEAD_PAD).astype(BF16)
        wv = _pad_cols(wkv[..., NOPE:], HEAD_PAD).reshape(KV_RANK, HEADS * HEAD_PAD).astype(BF16)

        q, k, v, loc = _proj_call(xs, pos, b, l, row(mix_norm[i]), win, row(mla_q_norm[i]), wqa, wqb,
                                  row(mla_kv_norm[i]), wk, wv, freq, sign, vone)
        y_mla = _attn_call(q, k, v).reshape(n, MLA_WIDTH)

        y_conv, hv, hx1, hx2 = _local_call(
            loc.reshape(b, l, LOC_COLS), conv_dw_w[i], row(conv_dw_b[i]), row(conv_ln_g[i]),
            row(conv_ln_b[i]), hy_short_w[i], row(hy_short_b[i]))

        w4 = hy_filt_w4[i].reshape(HY_HID, HY_ORDER, 2, HY_WIDTH)
        w4s = jnp.stack([_pad2(w4[:, :, s].reshape(HY_HID, HY_ORDER * HY_WIDTH), LANE, HY_ORDER * HY_WIDTH)
                         for s in range(2)])
        hrow = lambda v_: _pad2(v_.reshape(1, -1), 1, LANE)
        kext, ssum = _filter_call(
            zt, _pad2(hy_filt_w1[i], LANE, LANE), hrow(hy_filt_b1[i]), _pad2(hy_filt_w2[i], LANE, LANE),
            hrow(hy_filt_b2[i]), _pad2(hy_filt_w3[i], LANE, LANE), hrow(hy_filt_b3[i]), w4s,
            hrow(hy_filt_freq[i]), deltas, l)
        kf = _spectrum_call(kext, ssum, tf1, gf, l)
        dd = hy_bias_d[i].reshape(HY_ORDER, 1, HY_WIDTH)
        y1 = _longconv_call(hv, None, kf, dd, tf1, gf, gi, ti2, 0)
        y2 = _longconv_call(hx1, y1, kf, dd, tf1, gf, gi, ti2, 1)

        last = i == depth - 1
        xs = _ffn_call(xs, row(ffn2_norm[i]), ffn2_w_gate[i].astype(BF16), ffn2_w_up[i].astype(BF16),
                       ffn2_w_down[i].astype(BF16),
                       mix=(y_mla, y_conv.reshape(n, -1), y2.reshape(n, -1), hx2.reshape(n, -1),
                            row(out_norm[i]), w_out[i].astype(BF16)),
                       final_g=row(final_norm) if last else None)
    return xs.reshape(b, l, d)
```

```python
import functools
import math

import numpy as np
import jax
import jax.numpy as jnp
from jax import lax
from jax.experimental import pallas as pl
from jax.experimental.pallas import tpu as pltpu

F32 = jnp.float32
BF16 = jnp.bfloat16

D_MODEL = 1024
NORM_EPS = 1e-6
D_FF = 2816
HEADS = 8
NOPE = 64
ROPE = 32
VDIM = 64
QK_DIM = NOPE + ROPE
Q_RANK = 256
KV_RANK = 128
MLA_WIDTH = HEADS * VDIM
ROPE_THETA = 10000.0
CONV_WIDTH = 256
CONV_KERNEL = 31
HY_WIDTH = 256
HY_ORDER = 2
HY_SHORT = 3
HY_EMB = 33
HY_HID = 64
HY_FAST, HY_SLOW, HY_TARGET = 0.3, 1.5, 1e-2
OFF_Q = 0
OFF_KV = OFF_Q + Q_RANK
OFF_KPE = OFF_KV + KV_RANK
OFF_CONV = OFF_KPE + ROPE
OFF_HY = OFF_CONV + 2 * CONV_WIDTH
IN_COLS = OFF_HY + 3 * HY_WIDTH
LOC_COLS = 2 * CONV_WIDTH + 3 * HY_WIDTH

LANE = 128
HEAD_PAD = LANE
DFT_MINOR = 128
HALO = 16
LOCAL_ROWS = 64
DFT_UNROLL = 8
VMEM_LIMIT = 56 * 1024 * 1024


def _cparams(sem):
    return pltpu.CompilerParams(dimension_semantics=sem, vmem_limit_bytes=VMEM_LIMIT)


def _const_spec(shape):
    nd = len(shape)
    return pl.BlockSpec(shape, lambda *_: (0,) * nd, pipeline_mode=pl.Buffered(1))


def _rms(x, g):
    ms = jnp.mean(x * x, axis=-1, keepdims=True)
    return x * lax.rsqrt(ms + NORM_EPS) * g


def _dot(a, b):
    return jnp.dot(a, b, preferred_element_type=F32)


def _ffn_body(*refs, mix, final, ff_chunk):
    refs = list(refs)
    x_ref = refs.pop(0)
    if mix:
        ym_ref, yc_ref, yh_ref, x2_ref, on_ref, wo_ref = refs[:6]
        refs = refs[6:]
    g_ref, wg_ref, wu_ref, wd_ref = refs[:4]
    refs = refs[4:]
    if final:
        fin_ref = refs.pop(0)
    o_ref, a_scr = refs

    x = x_ref[...]
    if mix:
        on = on_ref[...]
        e1, e2 = MLA_WIDTH, MLA_WIDTH + CONV_WIDTH
        y = jnp.concatenate([
            _rms(ym_ref[...], on[:, :e1]),
            _rms(yc_ref[...], on[:, e1:e2]),
            _rms(yh_ref[...] * x2_ref[...], on[:, e2:]),
        ], axis=-1).astype(BF16)
        x = x + _dot(y, wo_ref[...])
    h = _rms(x, g_ref[...]).astype(BF16)
    for c in range(D_FF // ff_chunk):
        sl = slice(c * ff_chunk, (c + 1) * ff_chunk)
        g = _dot(h, wg_ref[:, sl])
        u = _dot(h, wu_ref[:, sl])
        a_scr[:, sl] = (g * jax.nn.sigmoid(g) * u).astype(BF16)
    y = x + 0.5 * _dot(a_scr[...], wd_ref[...])
    if final:
        y = _rms(y, fin_ref[...])
    o_ref[...] = y


def _ffn_call(x, norm_g, wg, wu, wd, mix=None, final_g=None, tm=512, ff_chunk=256):
    n, d = x.shape
    tm = min(tm, n)
    row = lambda w: pl.BlockSpec((tm, w), lambda i: (i, 0))
    args, specs = [x], [row(d)]
    if mix is not None:
        ym, yc, yh, x2, on, wo = mix
        args += [ym, yc, yh, x2, on, wo]
        specs += [row(ym.shape[1]), row(yc.shape[1]), row(yh.shape[1]), row(x2.shape[1]),
                  _const_spec(on.shape), _const_spec(wo.shape)]
    args += [norm_g, wg, wu, wd]
    specs += [_const_spec(norm_g.shape), _const_spec(wg.shape), _const_spec(wu.shape), _const_spec(wd.shape)]
    if final_g is not None:
        args.append(final_g)
        specs.append(_const_spec(final_g.shape))
    return pl.pallas_call(
        functools.partial(_ffn_body, mix=mix is not None, final=final_g is not None, ff_chunk=ff_chunk),
        grid=(n // tm,),
        in_specs=specs,
        out_specs=row(d),
        out_shape=jax.ShapeDtypeStruct((n, d), F32),
        scratch_shapes=[pltpu.VMEM((tm, D_FF), BF16)],
        compiler_params=_cparams(("parallel",)),
        name="ffn_mix" if mix is not None else "ffn",
    )(*args)


def _proj_body(x_ref, pos_ref, g_ref, win_ref, qn_ref, wqa_ref, wqb_ref, kvn_ref, wk_ref, wv_ref,
               freq_ref, sign_ref, vone_ref, q_ref, kt_ref, v_ref, loc_ref):
    xn = _rms(x_ref[...], g_ref[...]).astype(BF16)
    h = _dot(xn, win_ref[...])
    ang = pos_ref[...] * freq_ref[...]
    cos = jnp.cos(ang)
    sin = jnp.sin(ang) * sign_ref[...]
    o1 = Q_RANK
    o2 = o1 + KV_RANK
    o3 = o2 + HEAD_PAD
    o4 = o3 + HEAD_PAD
    cq = _rms(h[:, :o1], qn_ref[...]).astype(BF16)
    ckv = _rms(h[:, o1:o2], kvn_ref[...]).astype(BF16)
    kpe = h[:, o2:o3] * cos + h[:, o3:o4] * sin
    loc_ref[...] = h[:, o4:]
    qa = _dot(cq, wqa_ref[...])
    qb = _dot(cq, wqb_ref[...])
    kk = _dot(ckv, wk_ref[...])
    vv = _dot(ckv, wv_ref[...])
    scale = QK_DIM ** -0.5 * math.log2(math.e)
    cs, ss = cos * scale, sin * scale
    vone = vone_ref[...]
    for hd in range(HEADS):
        sl = slice(hd * HEAD_PAD, (hd + 1) * HEAD_PAD)
        q_ref[0, hd] = (qa[:, sl] * cs + qb[:, sl] * ss).astype(BF16)
        kt_ref[0, hd] = (kk[:, sl] + kpe).T.astype(BF16)
        v_ref[0, hd] = (vv[:, sl] + vone).astype(BF16)


def _proj_call(x, pos, b, l, g, win, qn, wqa, wqb, kvn, wk, wv, freq, sign, vone, tm=512):
    n, d = x.shape
    tm = min(tm, l)
    nlt = l // tm
    head_spec = pl.BlockSpec((1, HEADS, tm, HEAD_PAD), lambda i: (i // nlt, 0, i % nlt, 0))
    head_shape = jax.ShapeDtypeStruct((b, HEADS, l, HEAD_PAD), BF16)
    kt_spec = pl.BlockSpec((1, HEADS, HEAD_PAD, tm), lambda i: (i // nlt, 0, 0, i % nlt))
    kt_shape = jax.ShapeDtypeStruct((b, HEADS, HEAD_PAD, l), BF16)
    consts = [g, win, qn, wqa, wqb, kvn, wk, wv, freq, sign, vone]
    return pl.pallas_call(
        _proj_body,
        grid=(n // tm,),
        in_specs=[pl.BlockSpec((tm, d), lambda i: (i, 0)), pl.BlockSpec((tm, 1), lambda i: (i, 0))]
        + [_const_spec(c.shape) for c in consts],
        out_specs=[head_spec, kt_spec, head_spec, pl.BlockSpec((tm, LOC_COLS), lambda i: (i, 0))],
        out_shape=[head_shape, kt_shape, head_shape, jax.ShapeDtypeStruct((n, LOC_COLS), F32)],
        compiler_params=_cparams(("parallel",)),
        name="in_proj",
    )(x, pos, *consts)


def _attn_body(q_ref, kt_ref, v_ref, o_ref, *, tk, heads_per_step):
    l = kt_ref.shape[3]
    tq = q_ref.shape[2]
    hps = heads_per_step
    qs = [q_ref[0, hd] for hd in range(hps)]

    def step(j, carry):
        off = pl.multiple_of(j * tk, tk)
        new = []
        for hd in range(hps):
            m, acc = carry[hd]
            kt = kt_ref[0, hd, :, pl.ds(off, tk)]
            v = v_ref[0, hd, pl.ds(off, tk), :]
            s = _dot(qs[hd], kt)
            m_new = jnp.maximum(m, jnp.max(s, axis=-1, keepdims=True))
            alpha = jnp.exp2(m - m_new)
            p = jnp.exp2(s - m_new).astype(BF16)
            new.append((m_new, alpha * acc + _dot(p, v)))
        return tuple(new)

    init = tuple((jnp.full((tq, 1), -1e30, F32), jnp.zeros((tq, HEAD_PAD), F32)) for _ in range(hps))
    fin = lax.fori_loop(0, l // tk, step, init, unroll=True)
    o_ref[0] = jnp.concatenate([acc[:, :VDIM] / acc[:, VDIM:VDIM + 1] for _, acc in fin], axis=-1)


def _attn_call(q, kt, v, tq=512, tk=2048, heads_per_step=2):
    b, hn, l, dp = q.shape
    tq, tk = min(tq, l), min(tk, l)
    hps = heads_per_step
    return pl.pallas_call(
        functools.partial(_attn_body, tk=tk, heads_per_step=hps),
        grid=(b, hn // hps, l // tq),
        in_specs=[pl.BlockSpec((1, hps, tq, dp), lambda bi, hi, qi: (bi, hi, qi, 0)),
                  pl.BlockSpec((1, hps, dp, l), lambda bi, hi, qi: (bi, hi, 0, 0)),
                  pl.BlockSpec((1, hps, l, dp), lambda bi, hi, qi: (bi, hi, 0, 0))],
        out_specs=pl.BlockSpec((1, tq, hps * VDIM), lambda bi, hi, qi: (bi, qi, hi)),
        out_shape=jax.ShapeDtypeStruct((b, l, hn * VDIM), F32),
        compiler_params=_cparams(("parallel", "parallel", "arbitrary")),
        name="mla_attn",
    )(q, kt, v)


def _local_body(cur_ref, prev_ref, next_ref, cw_ref, cb_ref, lg_ref, lb_ref, sw_ref, sb_ref,
                yc_ref, hv_ref, hx1_ref, hx2_ref, u_scr, s_scr):
    t = cur_ref.shape[1]
    i = pl.program_id(1)
    first = i == 0
    last = i == pl.num_programs(1) - 1
    cw = 2 * CONV_WIDTH

    def glu(w):
        return w[:, :CONV_WIDTH] * jax.nn.sigmoid(w[:, CONV_WIDTH:cw])

    prev = prev_ref[0]
    nxt = next_ref[0]
    cur = cur_ref[0]
    pmask = jnp.where(first, 0.0, 1.0)
    nmask = jnp.where(last, 0.0, 1.0)
    u_scr[0:HALO, :] = glu(prev) * pmask
    u_scr[HALO:HALO + t, :] = glu(cur)
    u_scr[HALO + t:, :] = glu(nxt) * nmask
    s_scr[0:HALO, :] = prev[:, cw:] * pmask
    s_scr[HALO:HALO + t, :] = cur[:, cw:]
    s_scr[HALO + t:, :] = nxt[:, cw:] * nmask

    pad = CONV_KERNEL // 2
    sp = HY_SHORT // 2
    rc = min(LOCAL_ROWS, t)
    for r0 in range(0, t, rc):
        acc = jnp.zeros((rc, CONV_WIDTH), F32) + cb_ref[...]
        for kk in range(CONV_KERNEL):
            acc = acc + u_scr[pl.ds(r0 + HALO - pad + kk, rc), :] * cw_ref[kk:kk + 1, :]
        mu = jnp.mean(acc, axis=-1, keepdims=True)
        cen = acc - mu
        var = jnp.mean(cen * cen, axis=-1, keepdims=True)
        yn = cen * lax.rsqrt(var + NORM_EPS) * lg_ref[...] + lb_ref[...]
        yc_ref[0, r0:r0 + rc, :] = yn * jax.nn.sigmoid(yn)

        for gi, o_ref in enumerate((hv_ref, hx1_ref, hx2_ref)):
            cs = slice(gi * HY_WIDTH, (gi + 1) * HY_WIDTH)
            hs = jnp.zeros((rc, HY_WIDTH), F32) + sb_ref[:, cs]
            for kk in range(HY_SHORT):
                hs = hs + s_scr[pl.ds(r0 + HALO - sp + kk, rc), cs] * sw_ref[kk:kk + 1, cs]
            o_ref[0, r0:r0 + rc, :] = hs


def _local_call(loc, cw, cb, lg, lb, sw, sb, t=512):
    b, l, c = loc.shape
    t = min(t, l)
    r = t // HALO
    nh = l // HALO
    consts = [cw, cb, lg, lb, sw, sb]
    out_spec = pl.BlockSpec((1, t, HY_WIDTH), lambda bi, i: (bi, i, 0))
    out_shape = jax.ShapeDtypeStruct((b, l, HY_WIDTH), F32)
    return pl.pallas_call(
        _local_body,
        grid=(b, l // t),
        in_specs=[pl.BlockSpec((1, t, c), lambda bi, i: (bi, i, 0)),
                  pl.BlockSpec((1, HALO, c), lambda bi, i: (bi, jnp.maximum(i * r - 1, 0), 0)),
                  pl.BlockSpec((1, HALO, c), lambda bi, i: (bi, jnp.minimum((i + 1) * r, nh - 1), 0))]
        + [_const_spec(a.shape) for a in consts],
        out_specs=[out_spec] * 4,
        out_shape=[out_shape] * 4,
        scratch_shapes=[pltpu.VMEM((t + 2 * HALO, CONV_WIDTH), F32),
                        pltpu.VMEM((t + 2 * HALO, 3 * HY_WIDTH), F32)],
        compiler_params=_cparams(("parallel", "parallel")),
        name="local_mixers",
    )(loc, loc, loc, *consts)


def _filter_body(z_ref, w1_ref, b1_ref, w2_ref, b2_ref, w3_ref, b3_ref, w4_ref, fr_ref, dl_ref,
                 k_ref, s_ref):
    hp = lax.Precision.HIGHEST
    z = z_ref[...]
    fr = fr_ref[...]
    hd = jnp.sin(fr * (jnp.dot(z, w1_ref[...], precision=hp, preferred_element_type=F32) + b1_ref[...]))
    hd = jnp.sin(fr * (jnp.dot(hd, w2_ref[...], precision=hp, preferred_element_type=F32) + b2_ref[...]))
    hd = jnp.sin(fr * (jnp.dot(hd, w3_ref[...], precision=hp, preferred_element_type=F32) + b3_ref[...]))
    h = jnp.dot(hd, w4_ref[0], precision=hp, preferred_element_type=F32)
    tcol = z[:, 0:1]
    mask = z[:, HY_EMB:HY_EMB + 1]
    win = jnp.exp(-tcol * dl_ref[...]) * mask

    @pl.when(pl.program_id(0) == 0)
    def _():
        s_ref[...] = jnp.zeros_like(s_ref)

    for o in range(HY_ORDER):
        ko = h[:, o * HY_WIDTH:(o + 1) * HY_WIDTH] * win
        k_ref[o] = ko
        s_ref[o] += jnp.broadcast_to(jnp.sum(jnp.abs(ko), axis=0, keepdims=True), s_ref.shape[1:])


def _filter_call(zt, w1, b1, w2, b2, w3, b3, w4, fr, dl, l, t=512):
    n2 = zt.shape[0]
    t = min(t, l)
    per_half = l // t
    consts_a = [w1, b1, w2, b2, w3, b3]
    return pl.pallas_call(
        _filter_body,
        grid=(n2 // t,),
        in_specs=[pl.BlockSpec((t, zt.shape[1]), lambda i: (i, 0))]
        + [_const_spec(a.shape) for a in consts_a]
        + [pl.BlockSpec((1,) + w4.shape[1:], lambda i: (i // per_half, 0, 0)),
           _const_spec(fr.shape), _const_spec(dl.shape)],
        out_specs=[pl.BlockSpec((HY_ORDER, t, HY_WIDTH), lambda i: (0, i, 0)),
                   pl.BlockSpec((HY_ORDER, 8, HY_WIDTH), lambda i: (0, 0, 0))],
        out_shape=[jax.ShapeDtypeStruct((HY_ORDER, n2, HY_WIDTH), F32),
                   jax.ShapeDtypeStruct((HY_ORDER, 8, HY_WIDTH), F32)],
        compiler_params=_cparams(("arbitrary",)),
        name="hyena_filter",
    )(zt, *consts_a, w4, fr, dl)


def _dft_dims(l):
    n1 = 2 * l // DFT_MINOR
    nh = n1 // 2
    h = nh + 1
    hp = -(-h // 8) * 8
    return n1, nh, h, hp


def _unroll(trips):
    return max(u for u in range(1, DFT_UNROLL + 1) if trips % u == 0)


def _fwd_stage1(load_slab, tf1_ref, x1_scr, hp):
    def body(n2, c):
        u = load_slab(n2).astype(BF16)
        x1 = _dot(tf1_ref[n2], u)
        x1_scr[pl.ds(pl.multiple_of(n2 * 2 * hp, 8), 2 * hp), :] = x1
        return c
    lax.fori_loop(0, DFT_MINOR, body, 0, unroll=_unroll(DFT_MINOR))


def _fwd_stage2(k1, gf_ref, x1_scr, hp):
    are = x1_scr[pl.ds(k1, DFT_MINOR, stride=2 * hp), :]
    aim = x1_scr[pl.ds(hp + k1, DFT_MINOR, stride=2 * hp), :]
    rhs = jnp.concatenate([are, aim], axis=0).astype(BF16)
    return _dot(gf_ref[...], rhs)


def _spectrum_body(kf_ref, kb_ref, s_ref, tf1_ref, gf_ref, o_ref, x1_scr, *, l):
    _, nh, h, hp = _dft_dims(l)
    inv = 1.0 / s_ref[0, 0:1, :]

    for half, ref in enumerate((kf_ref, kb_ref)):
        _fwd_stage1(lambda n2, ref=ref: ref[0, pl.ds(n2, nh, stride=DFT_MINOR), :], tf1_ref, x1_scr, hp)

        def body(k1, c, half=half):
            x = _fwd_stage2(k1, gf_ref, x1_scr, hp) * inv
            if half == 0:
                o_ref[0, k1] = x
            else:
                sgn = jnp.where(k1 % 2 == 0, 1.0, -1.0)
                o_ref[0, k1] = o_ref[0, k1] + sgn * x
            return c
        lax.fori_loop(0, h, body, 0, unroll=_unroll(h))


def _spectrum_call(kext, ssum, tf1, gf, l):
    _, nh, h, hp = _dft_dims(l)
    order, _, c = kext.shape
    cb = LANE
    return pl.pallas_call(
        functools.partial(_spectrum_body, l=l),
        grid=(order, c // cb),
        in_specs=[pl.BlockSpec((1, l, cb), lambda o, ci: (o, 0, ci), pipeline_mode=pl.Buffered(1)),
                  pl.BlockSpec((1, l, cb), lambda o, ci: (o, 1, ci), pipeline_mode=pl.Buffered(1)),
                  pl.BlockSpec((1, 8, cb), lambda o, ci: (o, 0, ci)),
                  _const_spec(tf1.shape), _const_spec(gf.shape)],
        out_specs=pl.BlockSpec((1, h, 2 * DFT_MINOR, cb), lambda o, ci: (o, 0, 0, ci),
                               pipeline_mode=pl.Buffered(1)),
        out_shape=jax.ShapeDtypeStruct((order, h, 2 * DFT_MINOR, c), F32),
        scratch_shapes=[pltpu.VMEM((DFT_MINOR * 2 * hp, cb), F32)],
        compiler_params=_cparams(("parallel", "parallel")),
        name="hyena_spectrum",
    )(kext, kext, ssum, tf1, gf)


def _longconv_body(*refs, gated, l):
    if gated:
        a_ref, b_ref, kf_ref, d_ref, tf1_ref, gf_ref, gi_ref, ti2_ref, o_ref, x1_scr, z_scr = refs
    else:
        a_ref, kf_ref, d_ref, tf1_ref, gf_ref, gi_ref, ti2_ref, o_ref, x1_scr, z_scr = refs
        b_ref = None
    _, nh, h, hp = _dft_dims(l)
    m = DFT_MINOR

    def load_slab(n2):
        u = a_ref[0, pl.ds(n2, nh, stride=m), :]
        if gated:
            u = u * b_ref[0, pl.ds(n2, nh, stride=m), :]
        return u

    _fwd_stage1(load_slab, tf1_ref, x1_scr, hp)

    def mid(k1, c):
        x = _fwd_stage2(k1, gf_ref, x1_scr, hp)
        xre, xim = x[:m], x[m:]
        kre = kf_ref[0, k1, 0:m, :]
        kim = kf_ref[0, k1, m:2 * m, :]
        y = jnp.concatenate([xre * kre - xim * kim, xre * kim + xim * kre], axis=0).astype(BF16)
        z_scr[pl.ds(pl.multiple_of(k1 * 2 * m, 2 * m), 2 * m), :] = _dot(gi_ref[...], y)
        return c
    lax.fori_loop(0, h, mid, 0, unroll=_unroll(h))
    if hp > h:
        z_scr[h * 2 * m:, :] = jnp.zeros(((hp - h) * 2 * m, z_scr.shape[1]), F32)

    def last(n2, c):
        zre = z_scr[pl.ds(n2, hp, stride=2 * m), :]
        zim = z_scr[pl.ds(m + n2, hp, stride=2 * m), :]
        rhs = jnp.concatenate([zre, zim], axis=0).astype(BF16)
        y = _dot(ti2_ref[n2], rhs)
        o_ref[0, pl.ds(n2, nh, stride=m), :] = y
        return c
    lax.fori_loop(0, m, last, 0, unroll=_unroll(m))

    rows = min(512, l)

    def epi(i, c):
        sl = pl.ds(pl.multiple_of(i * rows, rows), rows)
        u = a_ref[0, sl, :]
        if gated:
            u = u * b_ref[0, sl, :]
        o_ref[0, sl, :] = o_ref[0, sl, :] + u * d_ref[0]
        return c
    lax.fori_loop(0, l // rows, epi, 0)


def _longconv_call(a, b, kf, d, tf1, gf, gi, ti2, order):
    bn, l, c = a.shape
    _, nh, h, hp = _dft_dims(l)
    cb = LANE
    sig = pl.BlockSpec((1, l, cb), lambda ci, bi: (bi, 0, ci), pipeline_mode=pl.Buffered(1))
    args = [a] + ([b] if b is not None else [])
    specs = [sig] * len(args)
    args += [kf, d, tf1, gf, gi, ti2]
    specs += [pl.BlockSpec((1, h, 2 * DFT_MINOR, cb), lambda ci, bi: (order, 0, 0, ci),
                           pipeline_mode=pl.Buffered(1)),
              pl.BlockSpec((1, 1, cb), lambda ci, bi: (order, 0, ci)),
              _const_spec(tf1.shape), _const_spec(gf.shape), _const_spec(gi.shape), _const_spec(ti2.shape)]
    return pl.pallas_call(
        functools.partial(_longconv_body, gated=b is not None, l=l),
        grid=(c // cb, bn),
        in_specs=specs,
        out_specs=sig,
        out_shape=jax.ShapeDtypeStruct((bn, l, c), F32),
        scratch_shapes=[pltpu.VMEM((DFT_MINOR * 2 * hp, cb), F32), pltpu.VMEM((hp * 2 * DFT_MINOR, cb), F32)],
        compiler_params=_cparams(("parallel", "parallel")),
        name="hyena_longconv",
    )(*args)


@functools.lru_cache(maxsize=None)
def _dft_tables(l):
    n = 2 * l
    n1, nh, h, hp = _dft_dims(l)
    m = DFT_MINOR
    n2i = np.arange(m)[:, None, None]
    k1i = np.arange(h)[None, :, None]
    n1i = np.arange(nh)[None, None, :]
    ang = 2.0 * np.pi * ((k1i * (m * n1i + n2i)) % n) / n
    tf1 = np.zeros((m, 2 * hp, nh), np.float32)
    tf1[:, :h] = np.cos(ang)
    tf1[:, hp:hp + h] = -np.sin(ang)
    w = np.full((h,), 2.0)
    w[0] = 1.0
    w[-1] = 1.0
    ti2 = np.zeros((m, nh, 2 * hp), np.float32)
    ti2[:, :, :h] = np.transpose(np.cos(ang) * (w[None, :, None] / n), (0, 2, 1))
    ti2[:, :, hp:hp + h] = np.transpose(-np.sin(ang) * (w[None, :, None] / n), (0, 2, 1))
    th = 2.0 * np.pi * ((np.arange(m)[:, None] * np.arange(m)[None, :]) % m) / m
    cm, sm = np.cos(th), np.sin(th)
    gf = np.block([[cm, sm], [-sm, cm]]).astype(np.float32)
    gi = np.block([[cm, -sm], [sm, cm]]).astype(np.float32)
    return tf1, gf, gi, ti2


@functools.lru_cache(maxsize=None)
def _filter_tables(l):
    bands = (HY_EMB - 1) // 2
    t = np.linspace(0.0, 1.0, l)[:, None]
    ang = 2.0 * np.pi * np.arange(l)[:, None] / l
    fb = np.linspace(1e-4, bands - 1, bands)[None, :]
    z = np.concatenate([t, np.cos(fb * ang), -np.sin(fb * ang)], axis=-1)
    idx = np.concatenate([np.arange(l), [0], np.arange(l - 1, 0, -1)])
    zt = np.zeros((2 * l, LANE), np.float32)
    zt[:, :HY_EMB] = z[idx]
    zt[:, HY_EMB] = 1.0
    zt[l, HY_EMB] = 0.0
    max_decay = math.log(HY_TARGET) / HY_FAST
    min_decay = math.log(HY_TARGET) / HY_SLOW
    deltas = np.abs(np.linspace(min_decay, max_decay, HY_WIDTH))[None, :].astype(np.float32)
    return zt, deltas


def _rope_rows():
    inv_freq = 1.0 / (ROPE_THETA ** (np.arange(0, ROPE, 2, dtype=np.float32) / ROPE))
    half = ROPE // 2
    freq = np.zeros((1, HEAD_PAD), np.float32)
    sign = np.zeros((1, HEAD_PAD), np.float32)
    freq[0, NOPE:NOPE + half] = inv_freq
    freq[0, NOPE + half:NOPE + ROPE] = inv_freq
    sign[0, NOPE:NOPE + half] = -1.0
    sign[0, NOPE + half:NOPE + ROPE] = 1.0
    vone = np.zeros((1, HEAD_PAD), np.float32)
    vone[0, VDIM] = 1.0
    return freq, sign, vone


def _pad_cols(w, width, at=0):
    out = jnp.zeros(w.shape[:-1] + (width,), w.dtype)
    return out.at[..., at:at + w.shape[-1]].set(w)


def _swap_halves(w):
    half = w.shape[-1] // 2
    return jnp.concatenate([w[..., half:], w[..., :half]], axis=-1)


def _pad2(w, rows, cols):
    out = jnp.zeros((rows, cols), w.dtype)
    return out.at[:w.shape[0], :w.shape[1]].set(w)


def kernel(x, positions, ffn1_norm, ffn1_w_gate, ffn1_w_up, ffn1_w_down, mix_norm, w_in, mla_q_norm, mla_w_qb, mla_kv_norm, mla_w_kvb, conv_dw_w, conv_dw_b, conv_ln_g, conv_ln_b, hy_short_w, hy_short_b, hy_filt_w1, hy_filt_b1, hy_filt_w2, hy_filt_b2, hy_filt_w3, hy_filt_b3, hy_filt_w4, hy_filt_freq, hy_bias_d, out_norm, w_out, ffn2_norm, ffn2_w_gate, ffn2_w_up, ffn2_w_down, final_norm):
    b, l, d = x.shape
    n = b * l
    depth = w_in.shape[0]
    row = lambda v: v.reshape(1, -1).astype(F32)

    tf1, gf, gi, ti2 = (jnp.asarray(t).astype(BF16) for t in _dft_tables(l))
    zt_np, deltas_np = _filter_tables(l)
    zt, deltas = jnp.asarray(zt_np), jnp.asarray(deltas_np)
    freq, sign, vone = (jnp.asarray(t) for t in _rope_rows())
    pos = positions.astype(F32).reshape(n, 1)

    xs = x.reshape(n, d)
    for i in range(depth):
        xs = _ffn_call(xs, row(ffn1_norm[i]), ffn1_w_gate[i].astype(BF16), ffn1_w_up[i].astype(BF16),
                       ffn1_w_down[i].astype(BF16))

        wi = w_in[i]
        kpe_w = wi[:, OFF_KPE:OFF_CONV]
        win = jnp.concatenate([
            wi[:, OFF_Q:OFF_KV], wi[:, OFF_KV:OFF_KPE],
            _pad_cols(kpe_w, HEAD_PAD, NOPE), _pad_cols(_swap_halves(kpe_w), HEAD_PAD, NOPE),
            wi[:, OFF_CONV:]], axis=1).astype(BF16)
        wq = mla_w_qb[i].reshape(Q_RANK, HEADS, QK_DIM)
        wqa = _pad_cols(wq, HEAD_PAD).reshape(Q_RANK, HEADS * HEAD_PAD).astype(BF16)
        wq_sw = jnp.concatenate([jnp.zeros_like(wq[..., :NOPE]), _swap_halves(wq[..., NOPE:])], axis=-1)
        wqb = _pad_cols(wq_sw, HEAD_PAD).reshape(Q_RANK, HEADS * HEAD_PAD).astype(BF16)
        wkv = mla_w_kvb[i].reshape(KV_RANK, HEADS, NOPE + VDIM)
        wk = _pad_cols(wkv[..., :NOPE], HEAD_PAD).reshape(KV_RANK, HEADS * HEAD_PAD).astype(BF16)
        wv = _pad_cols(wkv[..., NOPE:], HEAD_PAD).reshape(KV_RANK, HEADS * HEAD_PAD).astype(BF16)

        q, k, v, loc = _proj_call(xs, pos, b, l, row(mix_norm[i]), win, row(mla_q_norm[i]), wqa, wqb,
                                  row(mla_kv_norm[i]), wk, wv, freq, sign, vone)
        y_mla = _attn_call(q, k, v).reshape(n, MLA_WIDTH)

        y_conv, hv, hx1, hx2 = _local_call(
            loc.reshape(b, l, LOC_COLS), conv_dw_w[i], row(conv_dw_b[i]), row(conv_ln_g[i]),
            row(conv_ln_b[i]), hy_short_w[i], row(hy_short_b[i]))

        w4 = hy_filt_w4[i].reshape(HY_HID, HY_ORDER, 2, HY_WIDTH)
        w4s = jnp.stack([_pad2(w4[:, :, s].reshape(HY_HID, HY_ORDER * HY_WIDTH), LANE, HY_ORDER * HY_WIDTH)
                         for s in range(2)])
        hrow = lambda v_: _pad2(v_.reshape(1, -1), 1, LANE)
        kext, ssum = _filter_call(
            zt, _pad2(hy_filt_w1[i], LANE, LANE), hrow(hy_filt_b1[i]), _pad2(hy_filt_w2[i], LANE, LANE),
            hrow(hy_filt_b2[i]), _pad2(hy_filt_w3[i], LANE, LANE), hrow(hy_filt_b3[i]), w4s,
            hrow(hy_filt_freq[i]), deltas, l)
        kf = _spectrum_call(kext, ssum, tf1, gf, l)
        dd = hy_bias_d[i].reshape(HY_ORDER, 1, HY_WIDTH)
        y1 = _longconv_call(hv, None, kf, dd, tf1, gf, gi, ti2, 0)
        y2 = _longconv_call(hx1, y1, kf, dd, tf1, gf, gi, ti2, 1)

        last = i == depth - 1
        xs = _ffn_call(xs, row(ffn2_norm[i]), ffn2_w_gate[i].astype(BF16), ffn2_w_up[i].astype(BF16),
                       ffn2_w_down[i].astype(BF16),
                       mix=(y_mla, y_conv.reshape(n, -1), y2.reshape(n, -1), hx2.reshape(n, -1),
                            row(out_norm[i]), w_out[i].astype(BF16)),
                       final_g=row(final_norm) if last else None)
    return xs.reshape(b, l, d)
```

```python
import functools
import math

import numpy as np
import jax
import jax.numpy as jnp
from jax import lax
from jax.experimental import pallas as pl
from jax.experimental.pallas import tpu as pltpu

F32 = jnp.float32
BF16 = jnp.bfloat16

D_MODEL = 1024
NORM_EPS = 1e-6
D_FF = 2816
HEADS = 8
NOPE = 64
ROPE = 32
VDIM = 64
QK_DIM = NOPE + ROPE
Q_RANK = 256
KV_RANK = 128
MLA_WIDTH = HEADS * VDIM
ROPE_THETA = 10000.0
CONV_WIDTH = 256
CONV_KERNEL = 31
HY_WIDTH = 256
HY_ORDER = 2
HY_SHORT = 3
HY_EMB = 33
HY_HID = 64
HY_FAST, HY_SLOW, HY_TARGET = 0.3, 1.5, 1e-2
OFF_Q = 0
OFF_KV = OFF_Q + Q_RANK
OFF_KPE = OFF_KV + KV_RANK
OFF_CONV = OFF_KPE + ROPE
OFF_HY = OFF_CONV + 2 * CONV_WIDTH
IN_COLS = OFF_HY + 3 * HY_WIDTH
LOC_COLS = 2 * CONV_WIDTH + 3 * HY_WIDTH

LANE = 128
HEAD_PAD = LANE
DFT_MINOR = 128
HALO = 16
LOCAL_ROWS = 64
DFT_UNROLL = 16
SUBLANES = 8


def _pitch(rows):
    q = -(-rows // SUBLANES)
    return (q | 1) * SUBLANES


SLAB_PITCH = _pitch(DFT_MINOR)


def _slab_rows(l):
    return (l // DFT_MINOR) * SLAB_PITCH


def _store_slabs(o_ref, lead, r0, val):
    j, off = divmod(r0, DFT_MINOR)
    base = j * SLAB_PITCH + off
    o_ref[lead + (slice(base, base + val.shape[0]), slice(None))] = val


def _zero_slab_pads(o_ref, lead, nslabs):
    pad = SLAB_PITCH - DFT_MINOR
    for j in range(nslabs):
        o_ref[lead + (slice(j * SLAB_PITCH + DFT_MINOR, (j + 1) * SLAB_PITCH), slice(None))] = (
            jnp.zeros((pad, o_ref.shape[-1]), o_ref.dtype))
VMEM_LIMIT = 56 * 1024 * 1024


def _cparams(sem):
    return pltpu.CompilerParams(dimension_semantics=sem, vmem_limit_bytes=VMEM_LIMIT)


def _const_spec(shape):
    nd = len(shape)
    return pl.BlockSpec(shape, lambda *_: (0,) * nd, pipeline_mode=pl.Buffered(1))


def _rms(x, g):
    ms = jnp.mean(x * x, axis=-1, keepdims=True)
    return x * lax.rsqrt(ms + NORM_EPS) * g


def _dot(a, b):
    return jnp.dot(a, b, preferred_element_type=F32)


def _ffn_body(*refs, mix, final, ff_chunk):
    refs = list(refs)
    x_ref = refs.pop(0)
    if mix:
        ym_ref, yc_ref, yh_ref, x2_ref, on_ref, wo_ref = refs[:6]
        refs = refs[6:]
    g_ref, wg_ref, wu_ref, wd_ref = refs[:4]
    refs = refs[4:]
    if final:
        fin_ref = refs.pop(0)
    o_ref, a_scr = refs

    x = x_ref[...]
    if mix:
        on = on_ref[...]
        e1, e2 = MLA_WIDTH, MLA_WIDTH + CONV_WIDTH
        yh = jnp.concatenate([yh_ref[j * SLAB_PITCH:j * SLAB_PITCH + DFT_MINOR, :]
                              for j in range(x.shape[0] // DFT_MINOR)], axis=0)
        y = jnp.concatenate([
            _rms(ym_ref[...], on[:, :e1]),
            _rms(yc_ref[...], on[:, e1:e2]),
            _rms(yh * x2_ref[...], on[:, e2:]),
        ], axis=-1).astype(BF16)
        x = x + _dot(y, wo_ref[...])
    h = _rms(x, g_ref[...]).astype(BF16)
    for c in range(D_FF // ff_chunk):
        sl = slice(c * ff_chunk, (c + 1) * ff_chunk)
        g = _dot(h, wg_ref[:, sl])
        u = _dot(h, wu_ref[:, sl])
        a_scr[:, sl] = (g * jax.nn.sigmoid(g) * u).astype(BF16)
    y = x + 0.5 * _dot(a_scr[...], wd_ref[...])
    if final:
        y = _rms(y, fin_ref[...])
    o_ref[...] = y


def _ffn_call(x, norm_g, wg, wu, wd, mix=None, final_g=None, tm=512, ff_chunk=256):
    n, d = x.shape
    tm = min(tm, n)
    row = lambda w: pl.BlockSpec((tm, w), lambda i: (i, 0))
    args, specs = [x], [row(d)]
    if mix is not None:
        ym, yc, yh, x2, on, wo = mix
        args += [ym, yc, yh, x2, on, wo]
        specs += [row(ym.shape[1]), row(yc.shape[1]),
                  pl.BlockSpec((_slab_rows(tm), yh.shape[1]), lambda i: (i, 0)), row(x2.shape[1]),
                  _const_spec(on.shape), _const_spec(wo.shape)]
    args += [norm_g, wg, wu, wd]
    specs += [_const_spec(norm_g.shape), _const_spec(wg.shape), _const_spec(wu.shape), _const_spec(wd.shape)]
    if final_g is not None:
        args.append(final_g)
        specs.append(_const_spec(final_g.shape))
    return pl.pallas_call(
        functools.partial(_ffn_body, mix=mix is not None, final=final_g is not None, ff_chunk=ff_chunk),
        grid=(n // tm,),
        in_specs=specs,
        out_specs=row(d),
        out_shape=jax.ShapeDtypeStruct((n, d), F32),
        scratch_shapes=[pltpu.VMEM((tm, D_FF), BF16)],
        compiler_params=_cparams(("parallel",)),
        name="ffn_mix" if mix is not None else "ffn",
    )(*args)


def _proj_body(x_ref, pos_ref, g_ref, win_ref, qn_ref, wqa_ref, wqb_ref, kvn_ref, wk_ref, wv_ref,
               freq_ref, sign_ref, vone_ref, q_ref, kt_ref, v_ref, loc_ref):
    xn = _rms(x_ref[...], g_ref[...]).astype(BF16)
    h = _dot(xn, win_ref[...])
    ang = pos_ref[...] * freq_ref[...]
    cos = jnp.cos(ang)
    sin = jnp.sin(ang) * sign_ref[...]
    o1 = Q_RANK
    o2 = o1 + KV_RANK
    o3 = o2 + HEAD_PAD
    o4 = o3 + HEAD_PAD
    cq = _rms(h[:, :o1], qn_ref[...]).astype(BF16)
    ckv = _rms(h[:, o1:o2], kvn_ref[...]).astype(BF16)
    kpe = h[:, o2:o3] * cos + h[:, o3:o4] * sin
    loc_ref[...] = h[:, o4:]
    qa = _dot(cq, wqa_ref[...])
    qb = _dot(cq, wqb_ref[...])
    kk = _dot(ckv, wk_ref[...])
    vv = _dot(ckv, wv_ref[...])
    scale = QK_DIM ** -0.5 * math.log2(math.e)
    cs, ss = cos * scale, sin * scale
    vone = vone_ref[...]
    for hd in range(HEADS):
        sl = slice(hd * HEAD_PAD, (hd + 1) * HEAD_PAD)
        q_ref[0, hd] = (qa[:, sl] * cs + qb[:, sl] * ss).astype(BF16)
        kt_ref[0, hd] = (kk[:, sl] + kpe).T.astype(BF16)
        v_ref[0, hd] = (vv[:, sl] + vone).astype(BF16)


def _proj_call(x, pos, b, l, g, win, qn, wqa, wqb, kvn, wk, wv, freq, sign, vone, tm=512):
    n, d = x.shape
    tm = min(tm, l)
    nlt = l // tm
    head_spec = pl.BlockSpec((1, HEADS, tm, HEAD_PAD), lambda i: (i // nlt, 0, i % nlt, 0))
    head_shape = jax.ShapeDtypeStruct((b, HEADS, l, HEAD_PAD), BF16)
    kt_spec = pl.BlockSpec((1, HEADS, HEAD_PAD, tm), lambda i: (i // nlt, 0, 0, i % nlt))
    kt_shape = jax.ShapeDtypeStruct((b, HEADS, HEAD_PAD, l), BF16)
    consts = [g, win, qn, wqa, wqb, kvn, wk, wv, freq, sign, vone]
    return pl.pallas_call(
        _proj_body,
        grid=(n // tm,),
        in_specs=[pl.BlockSpec((tm, d), lambda i: (i, 0)), pl.BlockSpec((tm, 1), lambda i: (i, 0))]
        + [_const_spec(c.shape) for c in consts],
        out_specs=[head_spec, kt_spec, head_spec, pl.BlockSpec((tm, LOC_COLS), lambda i: (i, 0))],
        out_shape=[head_shape, kt_shape, head_shape, jax.ShapeDtypeStruct((n, LOC_COLS), F32)],
        compiler_params=_cparams(("parallel",)),
        name="in_proj",
    )(x, pos, *consts)


def _attn_body(q_ref, kt_ref, v_ref, o_ref, *, tk, heads_per_step):
    l = kt_ref.shape[3]
    tq = q_ref.shape[2]
    hps = heads_per_step
    qs = [q_ref[0, hd] for hd in range(hps)]

    def step(j, carry):
        off = pl.multiple_of(j * tk, tk)
        new = []
        for hd in range(hps):
            m, acc = carry[hd]
            kt = kt_ref[0, hd, :, pl.ds(off, tk)]
            v = v_ref[0, hd, pl.ds(off, tk), :]
            s = _dot(qs[hd], kt)
            m_new = jnp.maximum(m, jnp.max(s, axis=-1, keepdims=True))
            alpha = jnp.exp2(m - m_new)
            p = jnp.exp2(s - m_new).astype(BF16)
            new.append((m_new, alpha * acc + _dot(p, v)))
        return tuple(new)

    init = tuple((jnp.full((tq, 1), -1e30, F32), jnp.zeros((tq, HEAD_PAD), F32)) for _ in range(hps))
    fin = lax.fori_loop(0, l // tk, step, init, unroll=True)
    o_ref[0] = jnp.concatenate([acc[:, :VDIM] / acc[:, VDIM:VDIM + 1] for _, acc in fin], axis=-1)


def _attn_call(q, kt, v, tq=512, tk=2048, heads_per_step=2):
    b, hn, l, dp = q.shape
    tq, tk = min(tq, l), min(tk, l)
    hps = heads_per_step
    return pl.pallas_call(
        functools.partial(_attn_body, tk=tk, heads_per_step=hps),
        grid=(b, hn // hps, l // tq),
        in_specs=[pl.BlockSpec((1, hps, tq, dp), lambda bi, hi, qi: (bi, hi, qi, 0)),
                  pl.BlockSpec((1, hps, dp, l), lambda bi, hi, qi: (bi, hi, 0, 0)),
                  pl.BlockSpec((1, hps, l, dp), lambda bi, hi, qi: (bi, hi, 0, 0))],
        out_specs=pl.BlockSpec((1, tq, hps * VDIM), lambda bi, hi, qi: (bi, qi, hi)),
        out_shape=jax.ShapeDtypeStruct((b, l, hn * VDIM), F32),
        compiler_params=_cparams(("parallel", "parallel", "arbitrary")),
        name="mla_attn",
    )(q, kt, v)


def _local_body(cur_ref, prev_ref, next_ref, cw_ref, cb_ref, lg_ref, lb_ref, sw_ref, sb_ref,
                yc_ref, hv_ref, hx1_ref, hx2_ref, u_scr, s_scr):
    t = cur_ref.shape[1]
    i = pl.program_id(1)
    first = i == 0
    last = i == pl.num_programs(1) - 1
    cw = 2 * CONV_WIDTH

    def glu(w):
        return w[:, :CONV_WIDTH] * jax.nn.sigmoid(w[:, CONV_WIDTH:cw])

    prev = prev_ref[0]
    nxt = next_ref[0]
    cur = cur_ref[0]
    pmask = jnp.where(first, 0.0, 1.0)
    nmask = jnp.where(last, 0.0, 1.0)
    u_scr[0:HALO, :] = glu(prev) * pmask
    u_scr[HALO:HALO + t, :] = glu(cur)
    u_scr[HALO + t:, :] = glu(nxt) * nmask
    s_scr[0:HALO, :] = prev[:, cw:] * pmask
    s_scr[HALO:HALO + t, :] = cur[:, cw:]
    s_scr[HALO + t:, :] = nxt[:, cw:] * nmask

    pad = CONV_KERNEL // 2
    sp = HY_SHORT // 2
    rc = min(LOCAL_ROWS, t)
    for r0 in range(0, t, rc):
        acc = jnp.zeros((rc, CONV_WIDTH), F32) + cb_ref[...]
        wins = {}
        for r in range(SUBLANES):
            part = None
            for kk in range(CONV_KERNEL):
                a, rr = divmod(HALO - pad + kk, SUBLANES)
                if rr != r:
                    continue
                if a not in wins:
                    wins[a] = u_scr[pl.ds(r0 + a * SUBLANES, rc + SUBLANES), :]
                term = wins[a] * cw_ref[kk:kk + 1, :]
                part = term if part is None else part + term
            if part is not None:
                acc = acc + part[r:r + rc]
        mu = jnp.mean(acc, axis=-1, keepdims=True)
        cen = acc - mu
        var = jnp.mean(cen * cen, axis=-1, keepdims=True)
        yn = cen * lax.rsqrt(var + NORM_EPS) * lg_ref[...] + lb_ref[...]
        yc_ref[0, r0:r0 + rc, :] = yn * jax.nn.sigmoid(yn)

        for gi, o_ref in enumerate((hv_ref, hx1_ref, hx2_ref)):
            cs = slice(gi * HY_WIDTH, (gi + 1) * HY_WIDTH)
            hs = jnp.zeros((rc, HY_WIDTH), F32) + sb_ref[:, cs]
            for kk in range(HY_SHORT):
                hs = hs + s_scr[pl.ds(r0 + HALO - sp + kk, rc), cs] * sw_ref[kk:kk + 1, cs]
            if o_ref is hx2_ref:
                o_ref[0, r0:r0 + rc, :] = hs
            else:
                _store_slabs(o_ref, (0,), r0, hs)
    for o_ref in (hv_ref, hx1_ref):
        _zero_slab_pads(o_ref, (0,), t // DFT_MINOR)


def _local_call(loc, cw, cb, lg, lb, sw, sb, t=512):
    b, l, c = loc.shape
    t = min(t, l)
    r = t // HALO
    nh = l // HALO
    consts = [cw, cb, lg, lb, sw, sb]
    out_spec = pl.BlockSpec((1, t, HY_WIDTH), lambda bi, i: (bi, i, 0))
    out_shape = jax.ShapeDtypeStruct((b, l, HY_WIDTH), F32)
    slab_spec = pl.BlockSpec((1, _slab_rows(t), HY_WIDTH), lambda bi, i: (bi, i, 0))
    slab_shape = jax.ShapeDtypeStruct((b, _slab_rows(l), HY_WIDTH), F32)
    return pl.pallas_call(
        _local_body,
        grid=(b, l // t),
        in_specs=[pl.BlockSpec((1, t, c), lambda bi, i: (bi, i, 0)),
                  pl.BlockSpec((1, HALO, c), lambda bi, i: (bi, jnp.maximum(i * r - 1, 0), 0)),
                  pl.BlockSpec((1, HALO, c), lambda bi, i: (bi, jnp.minimum((i + 1) * r, nh - 1), 0))]
        + [_const_spec(a.shape) for a in consts],
        out_specs=[out_spec, slab_spec, slab_spec, out_spec],
        out_shape=[out_shape, slab_shape, slab_shape, out_shape],
        scratch_shapes=[pltpu.VMEM((t + 2 * HALO, CONV_WIDTH), F32),
                        pltpu.VMEM((t + 2 * HALO, 3 * HY_WIDTH), F32)],
        compiler_params=_cparams(("parallel", "parallel")),
        name="local_mixers",
    )(loc, loc, loc, *consts)


def _filter_body(z_ref, w1_ref, b1_ref, w2_ref, b2_ref, w3_ref, b3_ref, w4_ref, fr_ref, dl_ref,
                 k_ref, s_ref):
    hp = lax.Precision.HIGHEST
    z = z_ref[...]
    fr = fr_ref[...]
    hd = jnp.sin(fr * (jnp.dot(z, w1_ref[...], precision=hp, preferred_element_type=F32) + b1_ref[...]))
    hd = jnp.sin(fr * (jnp.dot(hd, w2_ref[...], precision=hp, preferred_element_type=F32) + b2_ref[...]))
    hd = jnp.sin(fr * (jnp.dot(hd, w3_ref[...], precision=hp, preferred_element_type=F32) + b3_ref[...]))
    h = jnp.dot(hd, w4_ref[0], precision=hp, preferred_element_type=F32)
    tcol = z[:, 0:1]
    mask = z[:, HY_EMB:HY_EMB + 1]
    win = jnp.exp(-tcol * dl_ref[...]) * mask

    @pl.when(pl.program_id(0) == 0)
    def _():
        s_ref[...] = jnp.zeros_like(s_ref)

    for o in range(HY_ORDER):
        ko = h[:, o * HY_WIDTH:(o + 1) * HY_WIDTH] * win
        for r0 in range(0, ko.shape[0], DFT_MINOR):
            _store_slabs(k_ref, (o,), r0, ko[r0:r0 + DFT_MINOR])
        _zero_slab_pads(k_ref, (o,), ko.shape[0] // DFT_MINOR)
        s_ref[o] += jnp.broadcast_to(jnp.sum(jnp.abs(ko), axis=0, keepdims=True), s_ref.shape[1:])


def _filter_call(zt, w1, b1, w2, b2, w3, b3, w4, fr, dl, l, t=512):
    n2 = zt.shape[0]
    t = min(t, l)
    per_half = l // t
    consts_a = [w1, b1, w2, b2, w3, b3]
    return pl.pallas_call(
        _filter_body,
        grid=(n2 // t,),
        in_specs=[pl.BlockSpec((t, zt.shape[1]), lambda i: (i, 0))]
        + [_const_spec(a.shape) for a in consts_a]
        + [pl.BlockSpec((1,) + w4.shape[1:], lambda i: (i // per_half, 0, 0)),
           _const_spec(fr.shape), _const_spec(dl.shape)],
        out_specs=[pl.BlockSpec((HY_ORDER, _slab_rows(t), HY_WIDTH), lambda i: (0, i, 0)),
                   pl.BlockSpec((HY_ORDER, 8, HY_WIDTH), lambda i: (0, 0, 0))],
        out_shape=[jax.ShapeDtypeStruct((HY_ORDER, _slab_rows(n2), HY_WIDTH), F32),
                   jax.ShapeDtypeStruct((HY_ORDER, 8, HY_WIDTH), F32)],
        compiler_params=_cparams(("arbitrary",)),
        name="hyena_filter",
    )(zt, *consts_a, w4, fr, dl)


def _dft_dims(l):
    n1 = 2 * l // DFT_MINOR
    nh = n1 // 2
    h = nh + 1
    hp = -(-h // 8) * 8
    return n1, nh, h, hp


def _unroll(trips):
    return max(u for u in range(1, DFT_UNROLL + 1) if trips % u == 0)


def _fwd_stage1(load_slab, tf1_ref, x1_scr, hp):
    def body(n2, c):
        u = load_slab(n2).astype(BF16)
        x1 = _dot(tf1_ref[n2], u)
        x1_scr[pl.ds(pl.multiple_of(n2 * _pitch(2 * hp), SUBLANES), 2 * hp), :] = x1
        return c
    lax.fori_loop(0, DFT_MINOR, body, 0, unroll=_unroll(DFT_MINOR))


def _fwd_stage2(k1, gf_ref, x1_scr, hp):
    are = x1_scr[pl.ds(k1, DFT_MINOR, stride=_pitch(2 * hp)), :]
    aim = x1_scr[pl.ds(hp + k1, DFT_MINOR, stride=_pitch(2 * hp)), :]
    rhs = jnp.concatenate([are, aim], axis=0).astype(BF16)
    return _dot(gf_ref[...], rhs)


def _spectrum_body(kf_ref, kb_ref, s_ref, tf1_ref, gf_ref, o_ref, x1_scr, *, l):
    _, nh, h, hp = _dft_dims(l)
    inv = 1.0 / s_ref[0, 0:1, :]

    for half, ref in enumerate((kf_ref, kb_ref)):
        _fwd_stage1(lambda n2, ref=ref: ref[0, pl.ds(n2, nh, stride=SLAB_PITCH), :], tf1_ref, x1_scr, hp)

        def body(k1, c, half=half):
            x = _fwd_stage2(k1, gf_ref, x1_scr, hp) * inv
            if half == 0:
                o_ref[0, k1] = x
            else:
                sgn = jnp.where(k1 % 2 == 0, 1.0, -1.0)
                o_ref[0, k1] = o_ref[0, k1] + sgn * x
            return c
        lax.fori_loop(0, h, body, 0, unroll=_unroll(h))


def _spectrum_call(kext, ssum, tf1, gf, l):
    _, nh, h, hp = _dft_dims(l)
    order, _, c = kext.shape
    cb = LANE
    return pl.pallas_call(
        functools.partial(_spectrum_body, l=l),
        grid=(order, c // cb),
        in_specs=[pl.BlockSpec((1, _slab_rows(l), cb), lambda o, ci: (o, 0, ci), pipeline_mode=pl.Buffered(1)),
                  pl.BlockSpec((1, _slab_rows(l), cb), lambda o, ci: (o, 1, ci), pipeline_mode=pl.Buffered(1)),
                  pl.BlockSpec((1, 8, cb), lambda o, ci: (o, 0, ci)),
                  _const_spec(tf1.shape), _const_spec(gf.shape)],
        out_specs=pl.BlockSpec((1, h, 2 * DFT_MINOR, cb), lambda o, ci: (o, 0, 0, ci),
                               pipeline_mode=pl.Buffered(1)),
        out_shape=jax.ShapeDtypeStruct((order, h, 2 * DFT_MINOR, c), F32),
        scratch_shapes=[pltpu.VMEM((DFT_MINOR * _pitch(2 * hp), cb), F32)],
        compiler_params=_cparams(("parallel", "parallel")),
        name="hyena_spectrum",
    )(kext, kext, ssum, tf1, gf)


def _longconv_body(*refs, gated, l):
    if gated:
        a_ref, b_ref, kf_ref, d_ref, tf1_ref, gf_ref, gi_ref, ti2_ref, o_ref, x1_scr, z_scr = refs
    else:
        a_ref, kf_ref, d_ref, tf1_ref, gf_ref, gi_ref, ti2_ref, o_ref, x1_scr, z_scr = refs
        b_ref = None
    _, nh, h, hp = _dft_dims(l)
    m = DFT_MINOR

    zp = _pitch(2 * m)

    def load_slab(n2):
        u = a_ref[0, pl.ds(n2, nh, stride=SLAB_PITCH), :]
        if gated:
            u = u * b_ref[0, pl.ds(n2, nh, stride=SLAB_PITCH), :]
        return u

    _fwd_stage1(load_slab, tf1_ref, x1_scr, hp)

    def mid(k1, c):
        x = _fwd_stage2(k1, gf_ref, x1_scr, hp)
        xre, xim = x[:m], x[m:]
        kre = kf_ref[0, k1, 0:m, :]
        kim = kf_ref[0, k1, m:2 * m, :]
        y = jnp.concatenate([xre * kre - xim * kim, xre * kim + xim * kre], axis=0).astype(BF16)
        z_scr[pl.ds(pl.multiple_of(k1 * zp, SUBLANES), 2 * m), :] = _dot(gi_ref[...], y)
        return c
    lax.fori_loop(0, h, mid, 0, unroll=_unroll(h))
    for k1 in range(h, hp):
        z_scr[k1 * zp:k1 * zp + 2 * m, :] = jnp.zeros((2 * m, z_scr.shape[1]), F32)

    def last(n2, c):
        zre = z_scr[pl.ds(n2, hp, stride=zp), :]
        zim = z_scr[pl.ds(m + n2, hp, stride=zp), :]
        rhs = jnp.concatenate([zre, zim], axis=0).astype(BF16)
        y = _dot(ti2_ref[n2], rhs)
        o_ref[0, pl.ds(n2, nh, stride=SLAB_PITCH), :] = y
        return c
    lax.fori_loop(0, m, last, 0, unroll=_unroll(m))

    def epi(j, c):
        sl = pl.ds(pl.multiple_of(j * SLAB_PITCH, SUBLANES), m)
        u = a_ref[0, sl, :]
        if gated:
            u = u * b_ref[0, sl, :]
        o_ref[0, sl, :] = o_ref[0, sl, :] + u * d_ref[0]
        pad = pl.ds(pl.multiple_of(j * SLAB_PITCH + m, SUBLANES), SLAB_PITCH - m)
        o_ref[0, pad, :] = jnp.zeros((SLAB_PITCH - m, o_ref.shape[2]), F32)
        return c
    lax.fori_loop(0, nh, epi, 0, unroll=_unroll(nh))


def _longconv_call(a, b, kf, d, tf1, gf, gi, ti2, order, l):
    bn, _, c = a.shape
    _, nh, h, hp = _dft_dims(l)
    cb = LANE
    sig = pl.BlockSpec((1, _slab_rows(l), cb), lambda ci, bi: (bi, 0, ci), pipeline_mode=pl.Buffered(1))
    args = [a] + ([b] if b is not None else [])
    specs = [sig] * len(args)
    args += [kf, d, tf1, gf, gi, ti2]
    specs += [pl.BlockSpec((1, h, 2 * DFT_MINOR, cb), lambda ci, bi: (order, 0, 0, ci),
                           pipeline_mode=pl.Buffered(1)),
              pl.BlockSpec((1, 1, cb), lambda ci, bi: (order, 0, ci)),
              _const_spec(tf1.shape), _const_spec(gf.shape), _const_spec(gi.shape), _const_spec(ti2.shape)]
    return pl.pallas_call(
        functools.partial(_longconv_body, gated=b is not None, l=l),
        grid=(c // cb, bn),
        in_specs=specs,
        out_specs=sig,
        out_shape=jax.ShapeDtypeStruct((bn, _slab_rows(l), c), F32),
        scratch_shapes=[pltpu.VMEM((DFT_MINOR * _pitch(2 * hp), cb), F32),
                        pltpu.VMEM((hp * _pitch(2 * DFT_MINOR), cb), F32)],
        compiler_params=_cparams(("parallel", "parallel")),
        name="hyena_longconv",
    )(*args)


@functools.lru_cache(maxsize=None)
def _dft_tables(l):
    n = 2 * l
    n1, nh, h, hp = _dft_dims(l)
    m = DFT_MINOR
    n2i = np.arange(m)[:, None, None]
    k1i = np.arange(h)[None, :, None]
    n1i = np.arange(nh)[None, None, :]
    ang = 2.0 * np.pi * ((k1i * (m * n1i + n2i)) % n) / n
    tf1 = np.zeros((m, 2 * hp, nh), np.float32)
    tf1[:, :h] = np.cos(ang)
    tf1[:, hp:hp + h] = -np.sin(ang)
    w = np.full((h,), 2.0)
    w[0] = 1.0
    w[-1] = 1.0
    ti2 = np.zeros((m, nh, 2 * hp), np.float32)
    ti2[:, :, :h] = np.transpose(np.cos(ang) * (w[None, :, None] / n), (0, 2, 1))
    ti2[:, :, hp:hp + h] = np.transpose(-np.sin(ang) * (w[None, :, None] / n), (0, 2, 1))
    th = 2.0 * np.pi * ((np.arange(m)[:, None] * np.arange(m)[None, :]) % m) / m
    cm, sm = np.cos(th), np.sin(th)
    gf = np.block([[cm, sm], [-sm, cm]]).astype(np.float32)
    gi = np.block([[cm, -sm], [sm, cm]]).astype(np.float32)
    return tf1, gf, gi, ti2


@functools.lru_cache(maxsize=None)
def _filter_tables(l):
    bands = (HY_EMB - 1) // 2
    t = np.linspace(0.0, 1.0, l)[:, None]
    ang = 2.0 * np.pi * np.arange(l)[:, None] / l
    fb = np.linspace(1e-4, bands - 1, bands)[None, :]
    z = np.concatenate([t, np.cos(fb * ang), -np.sin(fb * ang)], axis=-1)
    idx = np.concatenate([np.arange(l), [0], np.arange(l - 1, 0, -1)])
    zt = np.zeros((2 * l, LANE), np.float32)
    zt[:, :HY_EMB] = z[idx]
    zt[:, HY_EMB] = 1.0
    zt[l, HY_EMB] = 0.0
    max_decay = math.log(HY_TARGET) / HY_FAST
    min_decay = math.log(HY_TARGET) / HY_SLOW
    deltas = np.abs(np.linspace(min_decay, max_decay, HY_WIDTH))[None, :].astype(np.float32)
    return zt, deltas


def _rope_rows():
    inv_freq = 1.0 / (ROPE_THETA ** (np.arange(0, ROPE, 2, dtype=np.float32) / ROPE))
    half = ROPE // 2
    freq = np.zeros((1, HEAD_PAD), np.float32)
    sign = np.zeros((1, HEAD_PAD), np.float32)
    freq[0, NOPE:NOPE + half] = inv_freq
    freq[0, NOPE + half:NOPE + ROPE] = inv_freq
    sign[0, NOPE:NOPE + half] = -1.0
    sign[0, NOPE + half:NOPE + ROPE] = 1.0
    vone = np.zeros((1, HEAD_PAD), np.float32)
    vone[0, VDIM] = 1.0
    return freq, sign, vone


def _pad_cols(w, width, at=0):
    out = jnp.zeros(w.shape[:-1] + (width,), w.dtype)
    return out.at[..., at:at + w.shape[-1]].set(w)


def _swap_halves(w):
    half = w.shape[-1] // 2
    return jnp.concatenate([w[..., half:], w[..., :half]], axis=-1)


def _pad2(w, rows, cols):
    out = jnp.zeros((rows, cols), w.dtype)
    return out.at[:w.shape[0], :w.shape[1]].set(w)


def kernel(x, positions, ffn1_norm, ffn1_w_gate, ffn1_w_up, ffn1_w_down, mix_norm, w_in, mla_q_norm, mla_w_qb, mla_kv_norm, mla_w_kvb, conv_dw_w, conv_dw_b, conv_ln_g, conv_ln_b, hy_short_w, hy_short_b, hy_filt_w1, hy_filt_b1, hy_filt_w2, hy_filt_b2, hy_filt_w3, hy_filt_b3, hy_filt_w4, hy_filt_freq, hy_bias_d, out_norm, w_out, ffn2_norm, ffn2_w_gate, ffn2_w_up, ffn2_w_down, final_norm):
    b, l, d = x.shape
    n = b * l
    depth = w_in.shape[0]
    row = lambda v: v.reshape(1, -1).astype(F32)

    tf1, gf, gi, ti2 = (jnp.asarray(t).astype(BF16) for t in _dft_tables(l))
    zt_np, deltas_np = _filter_tables(l)
    zt, deltas = jnp.asarray(zt_np), jnp.asarray(deltas_np)
    freq, sign, vone = (jnp.asarray(t) for t in _rope_rows())
    pos = positions.astype(F32).reshape(n, 1)

    xs = x.reshape(n, d)
    for i in range(depth):
        xs = _ffn_call(xs, row(ffn1_norm[i]), ffn1_w_gate[i].astype(BF16), ffn1_w_up[i].astype(BF16),
                       ffn1_w_down[i].astype(BF16))

        wi = w_in[i]
        kpe_w = wi[:, OFF_KPE:OFF_CONV]
        win = jnp.concatenate([
            wi[:, OFF_Q:OFF_KV], wi[:, OFF_KV:OFF_KPE],
            _pad_cols(kpe_w, HEAD_PAD, NOPE), _pad_cols(_swap_halves(kpe_w), HEAD_PAD, NOPE),
            wi[:, OFF_CONV:]], axis=1).astype(BF16)
        wq = mla_w_qb[i].reshape(Q_RANK, HEADS, QK_DIM)
        wqa = _pad_cols(wq, HEAD_PAD).reshape(Q_RANK, HEADS * HEAD_PAD).astype(BF16)
        wq_sw = jnp.concatenate([jnp.zeros_like(wq[..., :NOPE]), _swap_halves(wq[..., NOPE:])], axis=-1)
        wqb = _pad_cols(wq_sw, HEAD_PAD).reshape(Q_RANK, HEADS * HEAD_PAD).astype(BF16)
        wkv = mla_w_kvb[i].reshape(KV_RANK, HEADS, NOPE + VDIM)
        wk = _pad_cols(wkv[..., :NOPE], HEAD_PAD).reshape(KV_RANK, HEADS * HEAD_PAD).astype(BF16)
        wv = _pad_cols(wkv[..., NOPE:], HEAD_PAD).reshape(KV_RANK, HEADS * HEAD_PAD).astype(BF16)

        q, k, v, loc = _proj_call(xs, pos, b, l, row(mix_norm[i]), win, row(mla_q_norm[i]), wqa, wqb,
                                  row(mla_kv_norm[i]), wk, wv, freq, sign, vone)
        y_mla = _attn_call(q, k, v).reshape(n, MLA_WIDTH)

        y_conv, hv, hx1, hx2 = _local_call(
            loc.reshape(b, l, LOC_COLS), conv_dw_w[i], row(conv_dw_b[i]), row(conv_ln_g[i]),
            row(conv_ln_b[i]), hy_short_w[i], row(hy_short_b[i]))

        w4 = hy_filt_w4[i].reshape(HY_HID, HY_ORDER, 2, HY_WIDTH)
        w4s = jnp.stack([_pad2(w4[:, :, s].reshape(HY_HID, HY_ORDER * HY_WIDTH), LANE, HY_ORDER * HY_WIDTH)
                         for s in range(2)])
        hrow = lambda v_: _pad2(v_.reshape(1, -1), 1, LANE)
        kext, ssum = _filter_call(
            zt, _pad2(hy_filt_w1[i], LANE, LANE), hrow(hy_filt_b1[i]), _pad2(hy_filt_w2[i], LANE, LANE),
            hrow(hy_filt_b2[i]), _pad2(hy_filt_w3[i], LANE, LANE), hrow(hy_filt_b3[i]), w4s,
            hrow(hy_filt_freq[i]), deltas, l)
        kf = _spectrum_call(kext, ssum, tf1, gf, l)
        dd = hy_bias_d[i].reshape(HY_ORDER, 1, HY_WIDTH)
        y1 = _longconv_call(hv, None, kf, dd, tf1, gf, gi, ti2, 0, l)
        y2 = _longconv_call(hx1, y1, kf, dd, tf1, gf, gi, ti2, 1, l)

        last = i == depth - 1
        xs = _ffn_call(xs, row(ffn2_norm[i]), ffn2_w_gate[i].astype(BF16), ffn2_w_up[i].astype(BF16),
                       ffn2_w_down[i].astype(BF16),
                       mix=(y_mla, y_conv.reshape(n, -1), y2.reshape(-1, HY_WIDTH), hx2.reshape(n, -1),
                            row(out_norm[i]), w_out[i].astype(BF16)),
                       final_g=row(final_norm) if last else None)
    return xs.reshape(b, l, d)
```

```python
import functools
import math

import numpy as np
import jax
import jax.numpy as jnp
from jax import lax
from jax.experimental import pallas as pl
from jax.experimental.pallas import tpu as pltpu

F32 = jnp.float32
BF16 = jnp.bfloat16

D_MODEL = 1024
NORM_EPS = 1e-6
D_FF = 2816
HEADS = 8
NOPE = 64
ROPE = 32
VDIM = 64
QK_DIM = NOPE + ROPE
Q_RANK = 256
KV_RANK = 128
MLA_WIDTH = HEADS * VDIM
ROPE_THETA = 10000.0
CONV_WIDTH = 256
CONV_KERNEL = 31
HY_WIDTH = 256
HY_ORDER = 2
HY_SHORT = 3
HY_EMB = 33
HY_HID = 64
HY_FAST, HY_SLOW, HY_TARGET = 0.3, 1.5, 1e-2
OFF_Q = 0
OFF_KV = OFF_Q + Q_RANK
OFF_KPE = OFF_KV + KV_RANK
OFF_CONV = OFF_KPE + ROPE
OFF_HY = OFF_CONV + 2 * CONV_WIDTH
IN_COLS = OFF_HY + 3 * HY_WIDTH
LOC_COLS = 2 * CONV_WIDTH + 3 * HY_WIDTH

LANE = 128
HEAD_PAD = LANE
DFT_MINOR = 128
HALO = 16
LOCAL_ROWS = 64
DFT_UNROLL = 16
SUBLANES = 8


def _pitch(rows):
    q = -(-rows // SUBLANES)
    return (q | 1) * SUBLANES


SLAB_PITCH = _pitch(DFT_MINOR)


def _slab_rows(l):
    return (l // DFT_MINOR) * SLAB_PITCH


def _store_slabs(o_ref, lead, r0, val):
    j, off = divmod(r0, DFT_MINOR)
    base = j * SLAB_PITCH + off
    o_ref[lead + (slice(base, base + val.shape[0]), slice(None))] = val


def _zero_slab_pads(o_ref, lead, nslabs):
    pad = SLAB_PITCH - DFT_MINOR
    for j in range(nslabs):
        o_ref[lead + (slice(j * SLAB_PITCH + DFT_MINOR, (j + 1) * SLAB_PITCH), slice(None))] = (
            jnp.zeros((pad, o_ref.shape[-1]), o_ref.dtype))
VMEM_LIMIT = 56 * 1024 * 1024


def _cparams(sem):
    return pltpu.CompilerParams(dimension_semantics=sem, vmem_limit_bytes=VMEM_LIMIT)


def _const_spec(shape):
    nd = len(shape)
    return pl.BlockSpec(shape, lambda *_: (0,) * nd, pipeline_mode=pl.Buffered(1))


def _rms(x, g):
    ms = jnp.mean(x * x, axis=-1, keepdims=True)
    return x * lax.rsqrt(ms + NORM_EPS) * g


def _dot(a, b):
    return jnp.dot(a, b, preferred_element_type=F32)


def _ffn_body(*refs, mix, final, ff_chunk):
    refs = list(refs)
    x_ref = refs.pop(0)
    if mix:
        ym_ref, yc_ref, yh_ref, x2_ref, on_ref, wo_ref = refs[:6]
        refs = refs[6:]
    g_ref, wg_ref, wu_ref, wd_ref = refs[:4]
    refs = refs[4:]
    if final:
        fin_ref = refs.pop(0)
    o_ref, a_scr = refs

    x = x_ref[...]
    if mix:
        on = on_ref[...]
        e1, e2 = MLA_WIDTH, MLA_WIDTH + CONV_WIDTH
        yh = jnp.concatenate([yh_ref[j * SLAB_PITCH:j * SLAB_PITCH + DFT_MINOR, :]
                              for j in range(x.shape[0] // DFT_MINOR)], axis=0)
        y = jnp.concatenate([
            _rms(ym_ref[...], on[:, :e1]),
            _rms(yc_ref[...], on[:, e1:e2]),
            _rms(yh * x2_ref[...], on[:, e2:]),
        ], axis=-1).astype(BF16)
        x = x + _dot(y, wo_ref[...])
    h = _rms(x, g_ref[...]).astype(BF16)
    for c in range(D_FF // ff_chunk):
        sl = slice(c * ff_chunk, (c + 1) * ff_chunk)
        g = _dot(h, wg_ref[:, sl])
        u = _dot(h, wu_ref[:, sl])
        a_scr[:, sl] = (g * jax.nn.sigmoid(g) * u).astype(BF16)
    y = x + 0.5 * _dot(a_scr[...], wd_ref[...])
    if final:
        y = _rms(y, fin_ref[...])
    o_ref[...] = y


def _ffn_call(x, norm_g, wg, wu, wd, mix=None, final_g=None, tm=512, ff_chunk=256):
    n, d = x.shape
    tm = min(tm, n)
    row = lambda w: pl.BlockSpec((tm, w), lambda i: (i, 0))
    args, specs = [x], [row(d)]
    if mix is not None:
        ym, yc, yh, x2, on, wo = mix
        args += [ym, yc, yh, x2, on, wo]
        specs += [row(ym.shape[1]), row(yc.shape[1]),
                  pl.BlockSpec((_slab_rows(tm), yh.shape[1]), lambda i: (i, 0)), row(x2.shape[1]),
                  _const_spec(on.shape), _const_spec(wo.shape)]
    args += [norm_g, wg, wu, wd]
    specs += [_const_spec(norm_g.shape), _const_spec(wg.shape), _const_spec(wu.shape), _const_spec(wd.shape)]
    if final_g is not None:
        args.append(final_g)
        specs.append(_const_spec(final_g.shape))
    return pl.pallas_call(
        functools.partial(_ffn_body, mix=mix is not None, final=final_g is not None, ff_chunk=ff_chunk),
        grid=(n // tm,),
        in_specs=specs,
        out_specs=row(d),
        out_shape=jax.ShapeDtypeStruct((n, d), F32),
        scratch_shapes=[pltpu.VMEM((tm, D_FF), BF16)],
        compiler_params=_cparams(("parallel",)),
        name="ffn_mix" if mix is not None else "ffn",
    )(*args)


def _proj_body(x_ref, pos_ref, g_ref, win_ref, qn_ref, wq_ref, kvn_ref, wk_ref, wv_ref,
               freq_ref, lo_ref, hi_ref, vone_ref, q_ref, kt_ref, v_ref, loc_ref):
    xn = _rms(x_ref[...], g_ref[...]).astype(BF16)
    h = _dot(xn, win_ref[...])
    ang = pos_ref[...] * freq_ref[...]
    cos = jnp.cos(ang)
    sin = jnp.sin(ang)
    sin_lo = sin * lo_ref[...]
    sin_hi = sin * hi_ref[...]
    half = ROPE // 2

    def rope(t, c, s_lo, s_hi):
        return (t * c + pltpu.roll(t, HEAD_PAD - half, axis=1) * s_lo + pltpu.roll(t, half, axis=1) * s_hi)

    o1 = Q_RANK
    o2 = o1 + KV_RANK
    o3 = o2 + HEAD_PAD
    cq = _rms(h[:, :o1], qn_ref[...]).astype(BF16)
    ckv = _rms(h[:, o1:o2], kvn_ref[...]).astype(BF16)
    kpe = rope(h[:, o2:o3], cos, sin_lo, sin_hi)
    loc_ref[...] = h[:, o3:]
    qa = _dot(cq, wq_ref[...])
    kk = _dot(ckv, wk_ref[...])
    vv = _dot(ckv, wv_ref[...])
    scale = QK_DIM ** -0.5 * math.log2(math.e)
    cs, s_lo, s_hi = cos * scale, sin_lo * scale, sin_hi * scale
    vone = vone_ref[...]
    for hd in range(HEADS):
        sl = slice(hd * HEAD_PAD, (hd + 1) * HEAD_PAD)
        q_ref[0, hd] = rope(qa[:, sl], cs, s_lo, s_hi).astype(BF16)
        kt_ref[0, hd] = (kk[:, sl] + kpe).T.astype(BF16)
        v_ref[0, hd] = (vv[:, sl] + vone).astype(BF16)


def _proj_call(x, pos, b, l, g, win, qn, wq, kvn, wk, wv, freq, lo, hi, vone, tm=512):
    n, d = x.shape
    tm = min(tm, l)
    nlt = l // tm
    head_spec = pl.BlockSpec((1, HEADS, tm, HEAD_PAD), lambda i: (i // nlt, 0, i % nlt, 0))
    head_shape = jax.ShapeDtypeStruct((b, HEADS, l, HEAD_PAD), BF16)
    kt_spec = pl.BlockSpec((1, HEADS, HEAD_PAD, tm), lambda i: (i // nlt, 0, 0, i % nlt))
    kt_shape = jax.ShapeDtypeStruct((b, HEADS, HEAD_PAD, l), BF16)
    consts = [g, win, qn, wq, kvn, wk, wv, freq, lo, hi, vone]
    return pl.pallas_call(
        _proj_body,
        grid=(n // tm,),
        in_specs=[pl.BlockSpec((tm, d), lambda i: (i, 0)), pl.BlockSpec((tm, 1), lambda i: (i, 0))]
        + [_const_spec(c.shape) for c in consts],
        out_specs=[head_spec, kt_spec, head_spec, pl.BlockSpec((tm, LOC_COLS), lambda i: (i, 0))],
        out_shape=[head_shape, kt_shape, head_shape, jax.ShapeDtypeStruct((n, LOC_COLS), F32)],
        compiler_params=_cparams(("parallel",)),
        name="in_proj",
    )(x, pos, *consts)


def _attn_body(q_ref, kt_ref, v_ref, o_ref, *, tk, heads_per_step):
    l = kt_ref.shape[3]
    tq = q_ref.shape[2]
    hps = heads_per_step
    qs = [q_ref[0, hd] for hd in range(hps)]

    def step(j, carry):
        off = pl.multiple_of(j * tk, tk)
        new = []
        for hd in range(hps):
            m, acc = carry[hd]
            kt = kt_ref[0, hd, :, pl.ds(off, tk)]
            v = v_ref[0, hd, pl.ds(off, tk), :]
            s = _dot(qs[hd], kt)
            m_new = jnp.maximum(m, jnp.max(s, axis=-1, keepdims=True))
            alpha = jnp.exp2(m - m_new)
            p = jnp.exp2(s - m_new).astype(BF16)
            new.append((m_new, alpha * acc + _dot(p, v)))
        return tuple(new)

    init = tuple((jnp.full((tq, 1), -1e30, F32), jnp.zeros((tq, HEAD_PAD), F32)) for _ in range(hps))
    fin = lax.fori_loop(0, l // tk, step, init, unroll=True)
    o_ref[0] = jnp.concatenate([acc[:, :VDIM] / acc[:, VDIM:VDIM + 1] for _, acc in fin], axis=-1)


def _attn_call(q, kt, v, tq=1024, tk=2048, heads_per_step=2):
    b, hn, l, dp = q.shape
    tq, tk = min(tq, l), min(tk, l)
    hps = heads_per_step
    return pl.pallas_call(
        functools.partial(_attn_body, tk=tk, heads_per_step=hps),
        grid=(b, hn // hps, l // tq),
        in_specs=[pl.BlockSpec((1, hps, tq, dp), lambda bi, hi, qi: (bi, hi, qi, 0)),
                  pl.BlockSpec((1, hps, dp, l), lambda bi, hi, qi: (bi, hi, 0, 0)),
                  pl.BlockSpec((1, hps, l, dp), lambda bi, hi, qi: (bi, hi, 0, 0))],
        out_specs=pl.BlockSpec((1, tq, hps * VDIM), lambda bi, hi, qi: (bi, qi, hi)),
        out_shape=jax.ShapeDtypeStruct((b, l, hn * VDIM), F32),
        compiler_params=_cparams(("parallel", "parallel", "arbitrary")),
        name="mla_attn",
    )(q, kt, v)


def _local_body(cur_ref, prev_ref, next_ref, cw_ref, cb_ref, lg_ref, lb_ref, sw_ref, sb_ref,
                yc_ref, hv_ref, hx1_ref, hx2_ref, u_scr, s_scr):
    t = cur_ref.shape[1]
    i = pl.program_id(1)
    first = i == 0
    last = i == pl.num_programs(1) - 1
    cw = 2 * CONV_WIDTH

    def glu(w):
        return w[:, :CONV_WIDTH] * jax.nn.sigmoid(w[:, CONV_WIDTH:cw])

    prev = prev_ref[0]
    nxt = next_ref[0]
    cur = cur_ref[0]
    pmask = jnp.where(first, 0.0, 1.0)
    nmask = jnp.where(last, 0.0, 1.0)
    u_scr[0:HALO, :] = glu(prev) * pmask
    u_scr[HALO:HALO + t, :] = glu(cur)
    u_scr[HALO + t:, :] = glu(nxt) * nmask
    s_scr[0:HALO, :] = prev[:, cw:] * pmask
    s_scr[HALO:HALO + t, :] = cur[:, cw:]
    s_scr[HALO + t:, :] = nxt[:, cw:] * nmask

    pad = CONV_KERNEL // 2
    sp = HY_SHORT // 2
    rc = min(LOCAL_ROWS, t)
    for r0 in range(0, t, rc):
        acc = jnp.zeros((rc, CONV_WIDTH), F32) + cb_ref[...]
        wins = {}
        for r in range(SUBLANES):
            part = None
            for kk in range(CONV_KERNEL):
                a, rr = divmod(HALO - pad + kk, SUBLANES)
                if rr != r:
                    continue
                if a not in wins:
                    wins[a] = u_scr[pl.ds(r0 + a * SUBLANES, rc + SUBLANES), :]
                term = wins[a] * cw_ref[kk:kk + 1, :]
                part = term if part is None else part + term
            if part is not None:
                acc = acc + part[r:r + rc]
        mu = jnp.mean(acc, axis=-1, keepdims=True)
        cen = acc - mu
        var = jnp.mean(cen * cen, axis=-1, keepdims=True)
        yn = cen * lax.rsqrt(var + NORM_EPS) * lg_ref[...] + lb_ref[...]
        yc_ref[0, r0:r0 + rc, :] = yn * jax.nn.sigmoid(yn)

        for gi, o_ref in enumerate((hv_ref, hx1_ref, hx2_ref)):
            cs = slice(gi * HY_WIDTH, (gi + 1) * HY_WIDTH)
            hs = jnp.zeros((rc, HY_WIDTH), F32) + sb_ref[:, cs]
            for kk in range(HY_SHORT):
                hs = hs + s_scr[pl.ds(r0 + HALO - sp + kk, rc), cs] * sw_ref[kk:kk + 1, cs]
            if o_ref is hx2_ref:
                o_ref[0, r0:r0 + rc, :] = hs
            else:
                _store_slabs(o_ref, (0,), r0, hs)
    for o_ref in (hv_ref, hx1_ref):
        _zero_slab_pads(o_ref, (0,), t // DFT_MINOR)


def _local_call(loc, cw, cb, lg, lb, sw, sb, t=512):
    b, l, c = loc.shape
    t = min(t, l)
    r = t // HALO
    nh = l // HALO
    consts = [cw, cb, lg, lb, sw, sb]
    out_spec = pl.BlockSpec((1, t, HY_WIDTH), lambda bi, i: (bi, i, 0))
    out_shape = jax.ShapeDtypeStruct((b, l, HY_WIDTH), F32)
    slab_spec = pl.BlockSpec((1, _slab_rows(t), HY_WIDTH), lambda bi, i: (bi, i, 0))
    slab_shape = jax.ShapeDtypeStruct((b, _slab_rows(l), HY_WIDTH), F32)
    return pl.pallas_call(
        _local_body,
        grid=(b, l // t),
        in_specs=[pl.BlockSpec((1, t, c), lambda bi, i: (bi, i, 0)),
                  pl.BlockSpec((1, HALO, c), lambda bi, i: (bi, jnp.maximum(i * r - 1, 0), 0)),
                  pl.BlockSpec((1, HALO, c), lambda bi, i: (bi, jnp.minimum((i + 1) * r, nh - 1), 0))]
        + [_const_spec(a.shape) for a in consts],
        out_specs=[out_spec, slab_spec, slab_spec, out_spec],
        out_shape=[out_shape, slab_shape, slab_shape, out_shape],
        scratch_shapes=[pltpu.VMEM((t + 2 * HALO, CONV_WIDTH), F32),
                        pltpu.VMEM((t + 2 * HALO, 3 * HY_WIDTH), F32)],
        compiler_params=_cparams(("parallel", "parallel")),
        name="local_mixers",
    )(loc, loc, loc, *consts)


def _filter_body(z_ref, w1_ref, b1_ref, w2_ref, b2_ref, w3_ref, b3_ref, w4_ref, fr_ref, dl_ref,
                 kf_ref, kb_ref, s_ref):
    hp = lax.Precision.HIGHEST
    z = z_ref[...]
    fr = fr_ref[...]
    hd = jnp.sin(fr * (jnp.dot(z, w1_ref[...], precision=hp, preferred_element_type=F32) + b1_ref[...]))
    hd = jnp.sin(fr * (jnp.dot(hd, w2_ref[...], precision=hp, preferred_element_type=F32) + b2_ref[...]))
    hd = jnp.sin(fr * (jnp.dot(hd, w3_ref[...], precision=hp, preferred_element_type=F32) + b3_ref[...]))
    h = jnp.dot(hd, w4_ref[...], precision=hp, preferred_element_type=F32)
    tcol = z[:, 0:1]
    win = jnp.exp(-tcol * dl_ref[...])
    bwin = win * z[:, HY_EMB:HY_EMB + 1]

    @pl.when(pl.program_id(0) == 0)
    def _():
        s_ref[...] = jnp.zeros_like(s_ref)

    for o in range(HY_ORDER):
        asum = None
        for side, (k_ref, w) in enumerate(((kf_ref, win), (kb_ref, bwin))):
            c0 = (o * 2 + side) * HY_WIDTH
            ko = h[:, c0:c0 + HY_WIDTH] * w
            for r0 in range(0, ko.shape[0], DFT_MINOR):
                _store_slabs(k_ref, (o,), r0, ko[r0:r0 + DFT_MINOR])
            _zero_slab_pads(k_ref, (o,), ko.shape[0] // DFT_MINOR)
            part = jnp.sum(jnp.abs(ko), axis=0, keepdims=True)
            asum = part if asum is None else asum + part
        s_ref[o] += jnp.broadcast_to(asum, s_ref.shape[1:])


def _filter_call(zt, w1, b1, w2, b2, w3, b3, w4, fr, dl, t=512):
    l = zt.shape[0]
    t = min(t, l)
    consts = [w1, b1, w2, b2, w3, b3, w4, fr, dl]
    k_spec = pl.BlockSpec((HY_ORDER, _slab_rows(t), HY_WIDTH), lambda i: (0, i, 0))
    k_shape = jax.ShapeDtypeStruct((HY_ORDER, _slab_rows(l), HY_WIDTH), F32)
    return pl.pallas_call(
        _filter_body,
        grid=(l // t,),
        in_specs=[pl.BlockSpec((t, zt.shape[1]), lambda i: (i, 0))] + [_const_spec(a.shape) for a in consts],
        out_specs=[k_spec, k_spec, pl.BlockSpec((HY_ORDER, 8, HY_WIDTH), lambda i: (0, 0, 0))],
        out_shape=[k_shape, k_shape, jax.ShapeDtypeStruct((HY_ORDER, 8, HY_WIDTH), F32)],
        compiler_params=_cparams(("arbitrary",)),
        name="hyena_filter",
    )(zt, *consts)


def _dft_dims(l):
    n1 = 2 * l // DFT_MINOR
    nh = n1 // 2
    h = nh + 1
    hp = -(-h // 8) * 8
    return n1, nh, h, hp


def _unroll(trips):
    return max(u for u in range(1, DFT_UNROLL + 1) if trips % u == 0)


def _fwd_stage1(load_slab, tf1_ref, x1_scr, hp):
    def body(n2, c):
        u = load_slab(n2).astype(BF16)
        x1 = _dot(tf1_ref[n2], u)
        x1_scr[pl.ds(pl.multiple_of(n2 * _pitch(2 * hp), SUBLANES), 2 * hp), :] = x1
        return c
    lax.fori_loop(0, DFT_MINOR, body, 0, unroll=_unroll(DFT_MINOR))


def _fwd_stage2(k1, gf_ref, x1_scr, hp):
    are = x1_scr[pl.ds(k1, DFT_MINOR, stride=_pitch(2 * hp)), :]
    aim = x1_scr[pl.ds(hp + k1, DFT_MINOR, stride=_pitch(2 * hp)), :]
    rhs = jnp.concatenate([are, aim], axis=0).astype(BF16)
    return _dot(gf_ref[...], rhs)


def _spectrum_body(kf_ref, kb_ref, s_ref, tf1_ref, gf_ref, o_ref, x1_scr, *, l):
    _, nh, h, hp = _dft_dims(l)
    inv = 1.0 / s_ref[0, 0:1, :]

    for half, ref in enumerate((kf_ref, kb_ref)):
        _fwd_stage1(lambda n2, ref=ref: ref[0, pl.ds(n2, nh, stride=SLAB_PITCH), :], tf1_ref, x1_scr, hp)

        def body(k1, c, half=half):
            x = _fwd_stage2(k1, gf_ref, x1_scr, hp) * inv
            if half == 0:
                o_ref[0, k1] = x
            else:
                m = DFT_MINOR
                o_ref[0, k1, 0:m, :] = o_ref[0, k1, 0:m, :] + x[:m]
                o_ref[0, k1, m:, :] = o_ref[0, k1, m:, :] - x[m:]
            return c
        lax.fori_loop(0, h, body, 0, unroll=_unroll(h))


def _spectrum_call(kfwd, kbwd, ssum, tf1, gf, l):
    _, nh, h, hp = _dft_dims(l)
    order, _, c = kfwd.shape
    cb = LANE
    return pl.pallas_call(
        functools.partial(_spectrum_body, l=l),
        grid=(order, c // cb),
        in_specs=[pl.BlockSpec((1, _slab_rows(l), cb), lambda o, ci: (o, 0, ci), pipeline_mode=pl.Buffered(1)),
                  pl.BlockSpec((1, _slab_rows(l), cb), lambda o, ci: (o, 0, ci), pipeline_mode=pl.Buffered(1)),
                  pl.BlockSpec((1, 8, cb), lambda o, ci: (o, 0, ci)),
                  _const_spec(tf1.shape), _const_spec(gf.shape)],
        out_specs=pl.BlockSpec((1, h, 2 * DFT_MINOR, cb), lambda o, ci: (o, 0, 0, ci),
                               pipeline_mode=pl.Buffered(1)),
        out_shape=jax.ShapeDtypeStruct((order, h, 2 * DFT_MINOR, c), F32),
        scratch_shapes=[pltpu.VMEM((DFT_MINOR * _pitch(2 * hp), cb), F32)],
        compiler_params=_cparams(("parallel", "parallel")),
        name="hyena_spectrum",
    )(kfwd, kbwd, ssum, tf1, gf)


def _longconv_body(*refs, gated, l):
    if gated:
        a_ref, b_ref, kf_ref, d_ref, tf1_ref, gf_ref, gi_ref, ti2_ref, o_ref, x1_scr, z_scr = refs
    else:
        a_ref, kf_ref, d_ref, tf1_ref, gf_ref, gi_ref, ti2_ref, o_ref, x1_scr, z_scr = refs
        b_ref = None
    _, nh, h, hp = _dft_dims(l)
    m = DFT_MINOR

    zp = _pitch(2 * m)

    def load_slab(n2):
        u = a_ref[0, pl.ds(n2, nh, stride=SLAB_PITCH), :]
        if gated:
            u = u * b_ref[0, pl.ds(n2, nh, stride=SLAB_PITCH), :]
        return u

    _fwd_stage1(load_slab, tf1_ref, x1_scr, hp)

    def mid(k1, c):
        x = _fwd_stage2(k1, gf_ref, x1_scr, hp)
        xre, xim = x[:m], x[m:]
        kre = kf_ref[0, k1, 0:m, :]
        kim = kf_ref[0, k1, m:2 * m, :]
        y = jnp.concatenate([xre * kre - xim * kim, xre * kim + xim * kre], axis=0).astype(BF16)
        z_scr[pl.ds(pl.multiple_of(k1 * zp, SUBLANES), 2 * m), :] = _dot(gi_ref[...], y)
        return c
    lax.fori_loop(0, h, mid, 0, unroll=_unroll(h))
    for k1 in range(h, hp):
        z_scr[k1 * zp:k1 * zp + 2 * m, :] = jnp.zeros((2 * m, z_scr.shape[1]), F32)

    def last(n2, c):
        zre = z_scr[pl.ds(n2, hp, stride=zp), :]
        zim = z_scr[pl.ds(m + n2, hp, stride=zp), :]
        rhs = jnp.concatenate([zre, zim], axis=0).astype(BF16)
        y = _dot(ti2_ref[n2], rhs)
        o_ref[0, pl.ds(n2, nh, stride=SLAB_PITCH), :] = y
        return c
    lax.fori_loop(0, m, last, 0, unroll=_unroll(m))

    def epi(j, c):
        sl = pl.ds(pl.multiple_of(j * SLAB_PITCH, SUBLANES), m)
        u = a_ref[0, sl, :]
        if gated:
            u = u * b_ref[0, sl, :]
        o_ref[0, sl, :] = o_ref[0, sl, :] + u * d_ref[0]
        pad = pl.ds(pl.multiple_of(j * SLAB_PITCH + m, SUBLANES), SLAB_PITCH - m)
        o_ref[0, pad, :] = jnp.zeros((SLAB_PITCH - m, o_ref.shape[2]), F32)
        return c
    lax.fori_loop(0, nh, epi, 0, unroll=_unroll(nh))


def _longconv_call(a, b, kf, d, tf1, gf, gi, ti2, order, l):
    bn, _, c = a.shape
    _, nh, h, hp = _dft_dims(l)
    cb = LANE
    sig = pl.BlockSpec((1, _slab_rows(l), cb), lambda ci, bi: (bi, 0, ci), pipeline_mode=pl.Buffered(1))
    args = [a] + ([b] if b is not None else [])
    specs = [sig] * len(args)
    args += [kf, d, tf1, gf, gi, ti2]
    specs += [pl.BlockSpec((1, h, 2 * DFT_MINOR, cb), lambda ci, bi: (order, 0, 0, ci),
                           pipeline_mode=pl.Buffered(1)),
              pl.BlockSpec((1, 1, cb), lambda ci, bi: (order, 0, ci)),
              _const_spec(tf1.shape), _const_spec(gf.shape), _const_spec(gi.shape), _const_spec(ti2.shape)]
    return pl.pallas_call(
        functools.partial(_longconv_body, gated=b is not None, l=l),
        grid=(c // cb, bn),
        in_specs=specs,
        out_specs=pl.BlockSpec((1, _slab_rows(l), cb), lambda ci, bi: (bi, 0, ci)),
        out_shape=jax.ShapeDtypeStruct((bn, _slab_rows(l), c), F32),
        scratch_shapes=[pltpu.VMEM((DFT_MINOR * _pitch(2 * hp), cb), F32),
                        pltpu.VMEM((hp * _pitch(2 * DFT_MINOR), cb), F32)],
        compiler_params=_cparams(("parallel", "parallel")),
        name="hyena_longconv",
    )(*args)


@functools.lru_cache(maxsize=None)
def _dft_tables(l):
    n = 2 * l
    n1, nh, h, hp = _dft_dims(l)
    m = DFT_MINOR
    n2i = np.arange(m)[:, None, None]
    k1i = np.arange(h)[None, :, None]
    n1i = np.arange(nh)[None, None, :]
    ang = 2.0 * np.pi * ((k1i * (m * n1i + n2i)) % n) / n
    tf1 = np.zeros((m, 2 * hp, nh), np.float32)
    tf1[:, :h] = np.cos(ang)
    tf1[:, hp:hp + h] = -np.sin(ang)
    w = np.full((h,), 2.0)
    w[0] = 1.0
    w[-1] = 1.0
    ti2 = np.zeros((m, nh, 2 * hp), np.float32)
    ti2[:, :, :h] = np.transpose(np.cos(ang) * (w[None, :, None] / n), (0, 2, 1))
    ti2[:, :, hp:hp + h] = np.transpose(-np.sin(ang) * (w[None, :, None] / n), (0, 2, 1))
    th = 2.0 * np.pi * ((np.arange(m)[:, None] * np.arange(m)[None, :]) % m) / m
    cm, sm = np.cos(th), np.sin(th)
    gf = np.block([[cm, sm], [-sm, cm]]).astype(np.float32)
    gi = np.block([[cm, -sm], [sm, cm]]).astype(np.float32)
    return tf1, gf, gi, ti2


@functools.lru_cache(maxsize=None)
def _filter_tables(l):
    bands = (HY_EMB - 1) // 2
    t = np.linspace(0.0, 1.0, l)[:, None]
    ang = 2.0 * np.pi * np.arange(l)[:, None] / l
    fb = np.linspace(1e-4, bands - 1, bands)[None, :]
    z = np.concatenate([t, np.cos(fb * ang), -np.sin(fb * ang)], axis=-1)
    zt = np.zeros((l, LANE), np.float32)
    zt[:, :HY_EMB] = z
    zt[:, HY_EMB] = 1.0
    zt[0, HY_EMB] = 0.0
    max_decay = math.log(HY_TARGET) / HY_FAST
    min_decay = math.log(HY_TARGET) / HY_SLOW
    deltas = np.abs(np.linspace(min_decay, max_decay, HY_WIDTH))[None, :].astype(np.float32)
    return zt, deltas


def _rope_rows():
    inv_freq = 1.0 / (ROPE_THETA ** (np.arange(0, ROPE, 2, dtype=np.float32) / ROPE))
    half = ROPE // 2
    freq = np.zeros((1, HEAD_PAD), np.float32)
    lo = np.zeros((1, HEAD_PAD), np.float32)
    hi = np.zeros((1, HEAD_PAD), np.float32)
    freq[0, NOPE:NOPE + half] = inv_freq
    freq[0, NOPE + half:NOPE + ROPE] = inv_freq
    lo[0, NOPE:NOPE + half] = -1.0
    hi[0, NOPE + half:NOPE + ROPE] = 1.0
    vone = np.zeros((1, HEAD_PAD), np.float32)
    vone[0, VDIM] = 1.0
    return freq, lo, hi, vone


def _pad_cols(w, width, at=0):
    out = jnp.zeros(w.shape[:-1] + (width,), w.dtype)
    return out.at[..., at:at + w.shape[-1]].set(w)


def _swap_halves(w):
    half = w.shape[-1] // 2
    return jnp.concatenate([w[..., half:], w[..., :half]], axis=-1)


def _pad2(w, rows, cols):
    out = jnp.zeros((rows, cols), w.dtype)
    return out.at[:w.shape[0], :w.shape[1]].set(w)


def kernel(x, positions, ffn1_norm, ffn1_w_gate, ffn1_w_up, ffn1_w_down, mix_norm, w_in, mla_q_norm, mla_w_qb, mla_kv_norm, mla_w_kvb, conv_dw_w, conv_dw_b, conv_ln_g, conv_ln_b, hy_short_w, hy_short_b, hy_filt_w1, hy_filt_b1, hy_filt_w2, hy_filt_b2, hy_filt_w3, hy_filt_b3, hy_filt_w4, hy_filt_freq, hy_bias_d, out_norm, w_out, ffn2_norm, ffn2_w_gate, ffn2_w_up, ffn2_w_down, final_norm):
    b, l, d = x.shape
    n = b * l
    depth = w_in.shape[0]
    row = lambda v: v.reshape(1, -1).astype(F32)

    tf1, gf, gi, ti2 = (jnp.asarray(t).astype(BF16) for t in _dft_tables(l))
    zt_np, deltas_np = _filter_tables(l)
    zt, deltas = jnp.asarray(zt_np), jnp.asarray(deltas_np)
    freq, lo, hi, vone = (jnp.asarray(t) for t in _rope_rows())
    pos = positions.astype(F32).reshape(n, 1)

    xs = x.reshape(n, d)
    for i in range(depth):
        xs = _ffn_call(xs, row(ffn1_norm[i]), ffn1_w_gate[i].astype(BF16), ffn1_w_up[i].astype(BF16),
                       ffn1_w_down[i].astype(BF16))

        wi = w_in[i]
        win = jnp.concatenate([
            wi[:, OFF_Q:OFF_KV], wi[:, OFF_KV:OFF_KPE], _pad_cols(wi[:, OFF_KPE:OFF_CONV], HEAD_PAD, NOPE),
            wi[:, OFF_CONV:]], axis=1).astype(BF16)
        wq = _pad_cols(mla_w_qb[i].reshape(Q_RANK, HEADS, QK_DIM), HEAD_PAD)
        wq = wq.reshape(Q_RANK, HEADS * HEAD_PAD).astype(BF16)
        wkv = mla_w_kvb[i].reshape(KV_RANK, HEADS, NOPE + VDIM)
        wk = _pad_cols(wkv[..., :NOPE], HEAD_PAD).reshape(KV_RANK, HEADS * HEAD_PAD).astype(BF16)
        wv = _pad_cols(wkv[..., NOPE:], HEAD_PAD).reshape(KV_RANK, HEADS * HEAD_PAD).astype(BF16)

        q, k, v, loc = _proj_call(xs, pos, b, l, row(mix_norm[i]), win, row(mla_q_norm[i]), wq,
                                  row(mla_kv_norm[i]), wk, wv, freq, lo, hi, vone)
        y_mla = _attn_call(q, k, v).reshape(n, MLA_WIDTH)

        y_conv, hv, hx1, hx2 = _local_call(
            loc.reshape(b, l, LOC_COLS), conv_dw_w[i], row(conv_dw_b[i]), row(conv_ln_g[i]),
            row(conv_ln_b[i]), hy_short_w[i], row(hy_short_b[i]))

        hrow = lambda v_: _pad2(v_.reshape(1, -1), 1, LANE)
        kfwd, kbwd, ssum = _filter_call(
            zt, _pad2(hy_filt_w1[i], LANE, LANE), hrow(hy_filt_b1[i]), _pad2(hy_filt_w2[i], LANE, LANE),
            hrow(hy_filt_b2[i]), _pad2(hy_filt_w3[i], LANE, LANE), hrow(hy_filt_b3[i]),
            _pad2(hy_filt_w4[i], LANE, HY_ORDER * 2 * HY_WIDTH), hrow(hy_filt_freq[i]), deltas)
        kf = _spectrum_call(kfwd, kbwd, ssum, tf1, gf, l)
        dd = hy_bias_d[i].reshape(HY_ORDER, 1, HY_WIDTH)
        y1 = _longconv_call(hv, None, kf, dd, tf1, gf, gi, ti2, 0, l)
        y2 = _longconv_call(hx1, y1, kf, dd, tf1, gf, gi, ti2, 1, l)

        last = i == depth - 1
        xs = _ffn_call(xs, row(ffn2_norm[i]), ffn2_w_gate[i].astype(BF16), ffn2_w_up[i].astype(BF16),
                       ffn2_w_down[i].astype(BF16),
                       mix=(y_mla, y_conv.reshape(n, -1), y2.reshape(-1, HY_WIDTH), hx2.reshape(n, -1),
                            row(out_norm[i]), w_out[i].astype(BF16)),
                       final_g=row(final_norm) if last else None)
    return xs.reshape(b, l, d)
```

```python
import functools
import math

import numpy as np
import jax
import jax.numpy as jnp
from jax import lax
from jax.experimental import pallas as pl
from jax.experimental.pallas import tpu as pltpu

F32 = jnp.float32
BF16 = jnp.bfloat16

D_MODEL = 1024
NORM_EPS = 1e-6
D_FF = 2816
HEADS = 8
NOPE = 64
ROPE = 32
VDIM = 64
QK_DIM = NOPE + ROPE
Q_RANK = 256
KV_RANK = 128
MLA_WIDTH = HEADS * VDIM
ROPE_THETA = 10000.0
CONV_WIDTH = 256
CONV_KERNEL = 31
HY_WIDTH = 256
HY_ORDER = 2
HY_SHORT = 3
HY_EMB = 33
HY_HID = 64
HY_FAST, HY_SLOW, HY_TARGET = 0.3, 1.5, 1e-2
OFF_Q = 0
OFF_KV = OFF_Q + Q_RANK
OFF_KPE = OFF_KV + KV_RANK
OFF_CONV = OFF_KPE + ROPE
OFF_HY = OFF_CONV + 2 * CONV_WIDTH
IN_COLS = OFF_HY + 3 * HY_WIDTH
LOC_COLS = 2 * CONV_WIDTH + 3 * HY_WIDTH

LANE = 128
HEAD_PAD = LANE
DFT_MINOR = 128
HALO = 16
LOCAL_ROWS = 64
DFT_UNROLL = 16
SUBLANES = 8


def _pitch(rows):
    q = -(-rows // SUBLANES)
    return (q | 1) * SUBLANES


SLAB_PITCH = _pitch(DFT_MINOR)


def _slab_rows(l):
    return (l // DFT_MINOR) * SLAB_PITCH


def _store_slabs(o_ref, lead, r0, val):
    j, off = divmod(r0, DFT_MINOR)
    base = j * SLAB_PITCH + off
    o_ref[lead + (slice(base, base + val.shape[0]), slice(None))] = val


def _zero_slab_pads(o_ref, lead, nslabs):
    pad = SLAB_PITCH - DFT_MINOR
    for j in range(nslabs):
        o_ref[lead + (slice(j * SLAB_PITCH + DFT_MINOR, (j + 1) * SLAB_PITCH), slice(None))] = (
            jnp.zeros((pad, o_ref.shape[-1]), o_ref.dtype))
VMEM_LIMIT = 56 * 1024 * 1024


def _cparams(sem):
    return pltpu.CompilerParams(dimension_semantics=sem, vmem_limit_bytes=VMEM_LIMIT)


def _const_spec(shape):
    nd = len(shape)
    return pl.BlockSpec(shape, lambda *_: (0,) * nd, pipeline_mode=pl.Buffered(1))


def _rms(x, g):
    ms = jnp.mean(x * x, axis=-1, keepdims=True)
    return x * lax.rsqrt(ms + NORM_EPS) * g


def _dot(a, b):
    return jnp.dot(a, b, preferred_element_type=F32)


def _ffn_body(*refs, mix, final, ff_chunk):
    refs = list(refs)
    x_ref = refs.pop(0)
    if mix:
        ym_ref, yc_ref, yh_ref, x2_ref, on_ref, wo_ref = refs[:6]
        refs = refs[6:]
    g_ref, wg_ref, wu_ref, wd_ref = refs[:4]
    refs = refs[4:]
    if final:
        fin_ref = refs.pop(0)
    o_ref, a_scr = refs

    x = x_ref[...]
    if mix:
        on = on_ref[...]
        e1, e2 = MLA_WIDTH, MLA_WIDTH + CONV_WIDTH
        yh = jnp.concatenate([yh_ref[j * SLAB_PITCH:j * SLAB_PITCH + DFT_MINOR, :]
                              for j in range(x.shape[0] // DFT_MINOR)], axis=0)
        y = jnp.concatenate([
            _rms(ym_ref[...], on[:, :e1]),
            _rms(yc_ref[...], on[:, e1:e2]),
            _rms(yh * x2_ref[...], on[:, e2:]),
        ], axis=-1).astype(BF16)
        x = x + _dot(y, wo_ref[0])
    h = _rms(x, g_ref[...]).astype(BF16)
    for c in range(D_FF // ff_chunk):
        sl = slice(c * ff_chunk, (c + 1) * ff_chunk)
        g = _dot(h, wg_ref[0, :, sl])
        u = _dot(h, wu_ref[0, :, sl])
        a_scr[:, sl] = (g * jax.nn.sigmoid(g) * u).astype(BF16)
    y = x + 0.5 * _dot(a_scr[...], wd_ref[0])
    if final:
        y = _rms(y, fin_ref[...])
    o_ref[...] = y


def _cast_body(w_ref, o_ref):
    o_ref[...] = w_ref[...].astype(o_ref.dtype)


def _cast_call(w, rows=256):
    depth, r, c = w.shape
    rows = min(rows, r)
    spec = pl.BlockSpec((1, rows, c), lambda d, i: (d, i, 0))
    return pl.pallas_call(
        _cast_body,
        grid=(depth, r // rows),
        in_specs=[spec],
        out_specs=spec,
        out_shape=jax.ShapeDtypeStruct(w.shape, BF16),
        compiler_params=_cparams(("parallel", "parallel")),
        name="weight_cast",
    )(w)


def _layer_spec(w, layer):
    return pl.BlockSpec((1,) + w.shape[1:], lambda *_: (layer, 0, 0), pipeline_mode=pl.Buffered(1))


def _ffn_call(x, norm_g, wg, wu, wd, layer, mix=None, final_g=None, tm=512, ff_chunk=256):
    n, d = x.shape
    tm = min(tm, n)
    row = lambda w: pl.BlockSpec((tm, w), lambda i: (i, 0))
    args, specs = [x], [row(d)]
    if mix is not None:
        ym, yc, yh, x2, on, wo = mix
        args += [ym, yc, yh, x2, on, wo]
        specs += [row(ym.shape[1]), row(yc.shape[1]),
                  pl.BlockSpec((_slab_rows(tm), yh.shape[1]), lambda i: (i, 0)), row(x2.shape[1]),
                  _const_spec(on.shape), _layer_spec(wo, layer)]
    args += [norm_g, wg, wu, wd]
    specs += [_const_spec(norm_g.shape), _layer_spec(wg, layer), _layer_spec(wu, layer), _layer_spec(wd, layer)]
    if final_g is not None:
        args.append(final_g)
        specs.append(_const_spec(final_g.shape))
    return pl.pallas_call(
        functools.partial(_ffn_body, mix=mix is not None, final=final_g is not None, ff_chunk=ff_chunk),
        grid=(n // tm,),
        in_specs=specs,
        out_specs=row(d),
        out_shape=jax.ShapeDtypeStruct((n, d), F32),
        scratch_shapes=[pltpu.VMEM((tm, D_FF), BF16)],
        compiler_params=_cparams(("parallel",)),
        name="ffn_mix" if mix is not None else "ffn",
    )(*args)


def _proj_body(x_ref, pos_ref, g_ref, win_ref, qn_ref, wq_ref, kvn_ref, wk_ref, wv_ref,
               freq_ref, lo_ref, hi_ref, vone_ref, q_ref, kt_ref, v_ref, loc_ref):
    xn = _rms(x_ref[...], g_ref[...]).astype(BF16)
    h = _dot(xn, win_ref[...])
    ang = pos_ref[...] * freq_ref[...]
    cos = jnp.cos(ang)
    sin = jnp.sin(ang)
    sin_lo = sin * lo_ref[...]
    sin_hi = sin * hi_ref[...]
    half = ROPE // 2

    def rope(t, c, s_lo, s_hi):
        return (t * c + pltpu.roll(t, HEAD_PAD - half, axis=1) * s_lo + pltpu.roll(t, half, axis=1) * s_hi)

    o1 = Q_RANK
    o2 = o1 + KV_RANK
    o3 = o2 + HEAD_PAD
    cq = _rms(h[:, :o1], qn_ref[...]).astype(BF16)
    ckv = _rms(h[:, o1:o2], kvn_ref[...]).astype(BF16)
    kpe = rope(h[:, o2:o3], cos, sin_lo, sin_hi)
    loc_ref[...] = h[:, o3:]
    qa = _dot(cq, wq_ref[...])
    kk = _dot(ckv, wk_ref[...])
    vv = _dot(ckv, wv_ref[...])
    scale = QK_DIM ** -0.5 * math.log2(math.e)
    cs, s_lo, s_hi = cos * scale, sin_lo * scale, sin_hi * scale
    vone = vone_ref[...]
    for hd in range(HEADS):
        sl = slice(hd * HEAD_PAD, (hd + 1) * HEAD_PAD)
        q_ref[0, hd] = rope(qa[:, sl], cs, s_lo, s_hi).astype(BF16)
        kt_ref[0, hd] = (kk[:, sl] + kpe).T.astype(BF16)
        v_ref[0, hd] = (vv[:, sl] + vone).astype(BF16)


def _proj_call(x, pos, b, l, g, win, qn, wq, kvn, wk, wv, freq, lo, hi, vone, tm=512):
    n, d = x.shape
    tm = min(tm, l)
    nlt = l // tm
    head_spec = pl.BlockSpec((1, HEADS, tm, HEAD_PAD), lambda i: (i // nlt, 0, i % nlt, 0))
    head_shape = jax.ShapeDtypeStruct((b, HEADS, l, HEAD_PAD), BF16)
    kt_spec = pl.BlockSpec((1, HEADS, HEAD_PAD, tm), lambda i: (i // nlt, 0, 0, i % nlt))
    kt_shape = jax.ShapeDtypeStruct((b, HEADS, HEAD_PAD, l), BF16)
    consts = [g, win, qn, wq, kvn, wk, wv, freq, lo, hi, vone]
    return pl.pallas_call(
        _proj_body,
        grid=(n // tm,),
        in_specs=[pl.BlockSpec((tm, d), lambda i: (i, 0)), pl.BlockSpec((tm, 1), lambda i: (i, 0))]
        + [_const_spec(c.shape) for c in consts],
        out_specs=[head_spec, kt_spec, head_spec, pl.BlockSpec((tm, LOC_COLS), lambda i: (i, 0))],
        out_shape=[head_shape, kt_shape, head_shape, jax.ShapeDtypeStruct((n, LOC_COLS), F32)],
        compiler_params=_cparams(("parallel",)),
        name="in_proj",
    )(x, pos, *consts)


def _attn_body(q_ref, kt_ref, v_ref, o_ref, *, tk, heads_per_step):
    l = kt_ref.shape[3]
    tq = q_ref.shape[2]
    hps = heads_per_step
    qs = [q_ref[0, hd] for hd in range(hps)]

    def step(j, carry):
        off = pl.multiple_of(j * tk, tk)
        new = []
        for hd in range(hps):
            m, acc = carry[hd]
            kt = kt_ref[0, hd, :, pl.ds(off, tk)]
            v = v_ref[0, hd, pl.ds(off, tk), :]
            s = _dot(qs[hd], kt)
            m_new = jnp.maximum(m, jnp.max(s, axis=-1, keepdims=True))
            alpha = jnp.exp2(m - m_new)
            p = jnp.exp2(s - m_new).astype(BF16)
            new.append((m_new, alpha * acc + _dot(p, v)))
        return tuple(new)

    init = tuple((jnp.full((tq, 1), -1e30, F32), jnp.zeros((tq, HEAD_PAD), F32)) for _ in range(hps))
    fin = lax.fori_loop(0, l // tk, step, init, unroll=True)
    o_ref[0] = jnp.concatenate([acc[:, :VDIM] / acc[:, VDIM:VDIM + 1] for _, acc in fin], axis=-1)


def _attn_call(q, kt, v, tq=1024, tk=2048, heads_per_step=2):
    b, hn, l, dp = q.shape
    tq, tk = min(tq, l), min(tk, l)
    hps = heads_per_step
    return pl.pallas_call(
        functools.partial(_attn_body, tk=tk, heads_per_step=hps),
        grid=(b, hn // hps, l // tq),
        in_specs=[pl.BlockSpec((1, hps, tq, dp), lambda bi, hi, qi: (bi, hi, qi, 0)),
                  pl.BlockSpec((1, hps, dp, l), lambda bi, hi, qi: (bi, hi, 0, 0)),
                  pl.BlockSpec((1, hps, l, dp), lambda bi, hi, qi: (bi, hi, 0, 0))],
        out_specs=pl.BlockSpec((1, tq, hps * VDIM), lambda bi, hi, qi: (bi, qi, hi)),
        out_shape=jax.ShapeDtypeStruct((b, l, hn * VDIM), F32),
        compiler_params=_cparams(("parallel", "parallel", "arbitrary")),
        name="mla_attn",
    )(q, kt, v)


def _local_body(cur_ref, prev_ref, next_ref, cw_ref, cb_ref, lg_ref, lb_ref, sw_ref, sb_ref,
                yc_ref, hv_ref, hx1_ref, hx2_ref, u_scr, s_scr):
    t = cur_ref.shape[1]
    i = pl.program_id(1)
    first = i == 0
    last = i == pl.num_programs(1) - 1
    cw = 2 * CONV_WIDTH

    def glu(w):
        return w[:, :CONV_WIDTH] * jax.nn.sigmoid(w[:, CONV_WIDTH:cw])

    prev = prev_ref[0]
    nxt = next_ref[0]
    cur = cur_ref[0]
    pmask = jnp.where(first, 0.0, 1.0)
    nmask = jnp.where(last, 0.0, 1.0)
    u_scr[0:HALO, :] = glu(prev) * pmask
    u_scr[HALO:HALO + t, :] = glu(cur)
    u_scr[HALO + t:, :] = glu(nxt) * nmask
    s_scr[0:HALO, :] = prev[:, cw:] * pmask
    s_scr[HALO:HALO + t, :] = cur[:, cw:]
    s_scr[HALO + t:, :] = nxt[:, cw:] * nmask

    pad = CONV_KERNEL // 2
    sp = HY_SHORT // 2
    rc = min(LOCAL_ROWS, t)
    for r0 in range(0, t, rc):
        acc = jnp.zeros((rc, CONV_WIDTH), F32) + cb_ref[...]
        wins = {}
        for r in range(SUBLANES):
            part = None
            for kk in range(CONV_KERNEL):
                a, rr = divmod(HALO - pad + kk, SUBLANES)
                if rr != r:
                    continue
                if a not in wins:
                    wins[a] = u_scr[pl.ds(r0 + a * SUBLANES, rc + SUBLANES), :]
                term = wins[a] * cw_ref[kk:kk + 1, :]
                part = term if part is None else part + term
            if part is not None:
                acc = acc + part[r:r + rc]
        mu = jnp.mean(acc, axis=-1, keepdims=True)
        cen = acc - mu
        var = jnp.mean(cen * cen, axis=-1, keepdims=True)
        yn = cen * lax.rsqrt(var + NORM_EPS) * lg_ref[...] + lb_ref[...]
        yc_ref[0, r0:r0 + rc, :] = yn * jax.nn.sigmoid(yn)

        for gi, o_ref in enumerate((hv_ref, hx1_ref, hx2_ref)):
            cs = slice(gi * HY_WIDTH, (gi + 1) * HY_WIDTH)
            hs = jnp.zeros((rc, HY_WIDTH), F32) + sb_ref[:, cs]
            for kk in range(HY_SHORT):
                hs = hs + s_scr[pl.ds(r0 + HALO - sp + kk, rc), cs] * sw_ref[kk:kk + 1, cs]
            if o_ref is hx2_ref:
                o_ref[0, r0:r0 + rc, :] = hs
            else:
                _store_slabs(o_ref, (0,), r0, hs)
    for o_ref in (hv_ref, hx1_ref):
        _zero_slab_pads(o_ref, (0,), t // DFT_MINOR)


def _local_call(loc, cw, cb, lg, lb, sw, sb, t=512):
    b, l, c = loc.shape
    t = min(t, l)
    r = t // HALO
    nh = l // HALO
    consts = [cw, cb, lg, lb, sw, sb]
    out_spec = pl.BlockSpec((1, t, HY_WIDTH), lambda bi, i: (bi, i, 0))
    out_shape = jax.ShapeDtypeStruct((b, l, HY_WIDTH), F32)
    slab_spec = pl.BlockSpec((1, _slab_rows(t), HY_WIDTH), lambda bi, i: (bi, i, 0))
    slab_shape = jax.ShapeDtypeStruct((b, _slab_rows(l), HY_WIDTH), F32)
    return pl.pallas_call(
        _local_body,
        grid=(b, l // t),
        in_specs=[pl.BlockSpec((1, t, c), lambda bi, i: (bi, i, 0)),
                  pl.BlockSpec((1, HALO, c), lambda bi, i: (bi, jnp.maximum(i * r - 1, 0), 0)),
                  pl.BlockSpec((1, HALO, c), lambda bi, i: (bi, jnp.minimum((i + 1) * r, nh - 1), 0))]
        + [_const_spec(a.shape) for a in consts],
        out_specs=[out_spec, slab_spec, slab_spec, out_spec],
        out_shape=[out_shape, slab_shape, slab_shape, out_shape],
        scratch_shapes=[pltpu.VMEM((t + 2 * HALO, CONV_WIDTH), F32),
                        pltpu.VMEM((t + 2 * HALO, 3 * HY_WIDTH), F32)],
        compiler_params=_cparams(("parallel", "parallel")),
        name="local_mixers",
    )(loc, loc, loc, *consts)


def _filter_body(z_ref, w1_ref, b1_ref, w2_ref, b2_ref, w3_ref, b3_ref, w4_ref, fr_ref, dl_ref,
                 kf_ref, kb_ref, s_ref):
    hp = lax.Precision.HIGHEST
    z = z_ref[...]
    fr = fr_ref[...]
    hd = jnp.sin(fr * (jnp.dot(z, w1_ref[...], precision=hp, preferred_element_type=F32) + b1_ref[...]))
    hd = jnp.sin(fr * (jnp.dot(hd, w2_ref[...], precision=hp, preferred_element_type=F32) + b2_ref[...]))
    hd = jnp.sin(fr * (jnp.dot(hd, w3_ref[...], precision=hp, preferred_element_type=F32) + b3_ref[...]))
    h = jnp.dot(hd, w4_ref[...], precision=hp, preferred_element_type=F32)
    tcol = z[:, 0:1]
    win = jnp.exp(-tcol * dl_ref[...])
    bwin = win * z[:, HY_EMB:HY_EMB + 1]

    @pl.when(pl.program_id(0) == 0)
    def _():
        s_ref[...] = jnp.zeros_like(s_ref)

    for o in range(HY_ORDER):
        asum = None
        for side, (k_ref, w) in enumerate(((kf_ref, win), (kb_ref, bwin))):
            c0 = (o * 2 + side) * HY_WIDTH
            ko = h[:, c0:c0 + HY_WIDTH] * w
            for r0 in range(0, ko.shape[0], DFT_MINOR):
                _store_slabs(k_ref, (o,), r0, ko[r0:r0 + DFT_MINOR])
            _zero_slab_pads(k_ref, (o,), ko.shape[0] // DFT_MINOR)
            part = jnp.sum(jnp.abs(ko), axis=0, keepdims=True)
            asum = part if asum is None else asum + part
        s_ref[o] += jnp.broadcast_to(asum, s_ref.shape[1:])


def _filter_call(zt, w1, b1, w2, b2, w3, b3, w4, fr, dl, t=512):
    l = zt.shape[0]
    t = min(t, l)
    consts = [w1, b1, w2, b2, w3, b3, w4, fr, dl]
    k_spec = pl.BlockSpec((HY_ORDER, _slab_rows(t), HY_WIDTH), lambda i: (0, i, 0))
    k_shape = jax.ShapeDtypeStruct((HY_ORDER, _slab_rows(l), HY_WIDTH), F32)
    return pl.pallas_call(
        _filter_body,
        grid=(l // t,),
        in_specs=[pl.BlockSpec((t, zt.shape[1]), lambda i: (i, 0))] + [_const_spec(a.shape) for a in consts],
        out_specs=[k_spec, k_spec, pl.BlockSpec((HY_ORDER, 8, HY_WIDTH), lambda i: (0, 0, 0))],
        out_shape=[k_shape, k_shape, jax.ShapeDtypeStruct((HY_ORDER, 8, HY_WIDTH), F32)],
        compiler_params=_cparams(("arbitrary",)),
        name="hyena_filter",
    )(zt, *consts)


def _dft_dims(l):
    n1 = 2 * l // DFT_MINOR
    nh = n1 // 2
    h = nh + 1
    hp = -(-h // 8) * 8
    return n1, nh, h, hp


def _unroll(trips):
    return max(u for u in range(1, DFT_UNROLL + 1) if trips % u == 0)


def _fwd_stage1(load_slab, tf1_ref, x1_scr, hp):
    def body(n2, c):
        u = load_slab(n2).astype(BF16)
        x1 = _dot(tf1_ref[n2], u)
        x1_scr[pl.ds(pl.multiple_of(n2 * _pitch(2 * hp), SUBLANES), 2 * hp), :] = x1
        return c
    lax.fori_loop(0, DFT_MINOR, body, 0, unroll=_unroll(DFT_MINOR))


def _stage2_operand(k1, x1_scr, hp):
    are = x1_scr[pl.ds(k1, DFT_MINOR, stride=_pitch(2 * hp)), :]
    aim = x1_scr[pl.ds(hp + k1, DFT_MINOR, stride=_pitch(2 * hp)), :]
    return jnp.concatenate([are, aim], axis=0).astype(BF16)


def _fwd_stage2(k1, gf_ref, x1_scr, hp):
    return _dot(gf_ref[...], _stage2_operand(k1, x1_scr, hp))


def _spectrum_body(kf_ref, kb_ref, s_ref, tf1_ref, gf_ref, o_ref, x1_scr, *, l):
    _, nh, h, hp = _dft_dims(l)
    inv = 1.0 / s_ref[0, 0:1, :]

    for half, ref in enumerate((kf_ref, kb_ref)):
        _fwd_stage1(lambda n2, ref=ref: ref[0, pl.ds(n2, nh, stride=SLAB_PITCH), :], tf1_ref, x1_scr, hp)

        def body(k1, c, half=half):
            x = _fwd_stage2(k1, gf_ref, x1_scr, hp) * inv
            if half == 0:
                o_ref[0, k1] = x
            else:
                m = DFT_MINOR
                o_ref[0, k1, 0:m, :] = o_ref[0, k1, 0:m, :] + x[:m]
                o_ref[0, k1, m:, :] = o_ref[0, k1, m:, :] - x[m:]
            return c
        lax.fori_loop(0, hp, body, 0, unroll=_unroll(hp))


def _spectrum_call(kfwd, kbwd, ssum, tf1, gf, l):
    _, nh, h, hp = _dft_dims(l)
    order, _, c = kfwd.shape
    cb = LANE
    return pl.pallas_call(
        functools.partial(_spectrum_body, l=l),
        grid=(order, c // cb),
        in_specs=[pl.BlockSpec((1, _slab_rows(l), cb), lambda o, ci: (o, 0, ci), pipeline_mode=pl.Buffered(1)),
                  pl.BlockSpec((1, _slab_rows(l), cb), lambda o, ci: (o, 0, ci), pipeline_mode=pl.Buffered(1)),
                  pl.BlockSpec((1, 8, cb), lambda o, ci: (o, 0, ci)),
                  _const_spec(tf1.shape), _const_spec(gf.shape)],
        out_specs=pl.BlockSpec((1, hp, 2 * DFT_MINOR, cb), lambda o, ci: (o, 0, 0, ci),
                               pipeline_mode=pl.Buffered(1)),
        out_shape=jax.ShapeDtypeStruct((order, hp, 2 * DFT_MINOR, c), F32),
        scratch_shapes=[pltpu.VMEM((DFT_MINOR * _pitch(2 * hp), cb), F32)],
        compiler_params=_cparams(("parallel", "parallel")),
        name="hyena_spectrum",
    )(kfwd, kbwd, ssum, tf1, gf)


def _longconv_body(*refs, gated, l):
    if gated:
        a_ref, b_ref, kf_ref, d_ref, tf1_ref, gf_ref, gi_ref, ti2_ref, o_ref, x1_scr, z_scr = refs
    else:
        a_ref, kf_ref, d_ref, tf1_ref, gf_ref, gi_ref, ti2_ref, o_ref, x1_scr, z_scr = refs
        b_ref = None
    _, nh, h, hp = _dft_dims(l)
    m = DFT_MINOR

    zp = _pitch(2 * m)

    def load_slab(n2):
        u = a_ref[0, pl.ds(n2, nh, stride=SLAB_PITCH), :]
        if gated:
            u = u * b_ref[0, pl.ds(n2, nh, stride=SLAB_PITCH), :]
        return u

    _fwd_stage1(load_slab, tf1_ref, x1_scr, hp)

    cb = o_ref.shape[2]

    def mid(j, c):
        ks = (2 * j, 2 * j + 1)
        x = _dot(gf_ref[...], jnp.concatenate([_stage2_operand(k1, x1_scr, hp) for k1 in ks], axis=1))
        xre, xim = x[:m], x[m:]
        kre = jnp.concatenate([kf_ref[0, k1, 0:m, :] for k1 in ks], axis=1)
        kim = jnp.concatenate([kf_ref[0, k1, m:2 * m, :] for k1 in ks], axis=1)
        y = jnp.concatenate([xre * kre - xim * kim, xre * kim + xim * kre], axis=0).astype(BF16)
        z = _dot(gi_ref[...], y)
        for i, k1 in enumerate(ks):
            z_scr[pl.ds(pl.multiple_of(k1 * zp, SUBLANES), 2 * m), :] = z[:, i * cb:(i + 1) * cb]
        return c
    lax.fori_loop(0, hp // 2, mid, 0, unroll=_unroll(hp // 2))

    def last(n2, c):
        zre = z_scr[pl.ds(n2, hp, stride=zp), :]
        zim = z_scr[pl.ds(m + n2, hp, stride=zp), :]
        rhs = jnp.concatenate([zre, zim], axis=0).astype(BF16)
        y = _dot(ti2_ref[n2], rhs)
        o_ref[0, pl.ds(n2, nh, stride=SLAB_PITCH), :] = y
        return c
    lax.fori_loop(0, m, last, 0, unroll=_unroll(m))

    def epi(j, c):
        sl = pl.ds(pl.multiple_of(j * SLAB_PITCH, SUBLANES), m)
        u = a_ref[0, sl, :]
        if gated:
            u = u * b_ref[0, sl, :]
        o_ref[0, sl, :] = o_ref[0, sl, :] + u * d_ref[0]
        pad = pl.ds(pl.multiple_of(j * SLAB_PITCH + m, SUBLANES), SLAB_PITCH - m)
        o_ref[0, pad, :] = jnp.zeros((SLAB_PITCH - m, o_ref.shape[2]), F32)
        return c
    lax.fori_loop(0, nh, epi, 0, unroll=_unroll(nh))


def _longconv_call(a, b, kf, d, tf1, gf, gi, ti2, order, l):
    bn, _, c = a.shape
    _, nh, h, hp = _dft_dims(l)
    cb = LANE
    sig = pl.BlockSpec((1, _slab_rows(l), cb), lambda ci, bi: (bi, 0, ci), pipeline_mode=pl.Buffered(1))
    args = [a] + ([b] if b is not None else [])
    specs = [sig] * len(args)
    args += [kf, d, tf1, gf, gi, ti2]
    specs += [pl.BlockSpec((1, hp, 2 * DFT_MINOR, cb), lambda ci, bi: (order, 0, 0, ci),
                           pipeline_mode=pl.Buffered(1)),
              pl.BlockSpec((1, 1, cb), lambda ci, bi: (order, 0, ci)),
              _const_spec(tf1.shape), _const_spec(gf.shape), _const_spec(gi.shape), _const_spec(ti2.shape)]
    return pl.pallas_call(
        functools.partial(_longconv_body, gated=b is not None, l=l),
        grid=(c // cb, bn),
        in_specs=specs,
        out_specs=pl.BlockSpec((1, _slab_rows(l), cb), lambda ci, bi: (bi, 0, ci)),
        out_shape=jax.ShapeDtypeStruct((bn, _slab_rows(l), c), F32),
        scratch_shapes=[pltpu.VMEM((DFT_MINOR * _pitch(2 * hp), cb), F32),
                        pltpu.VMEM((hp * _pitch(2 * DFT_MINOR), cb), F32)],
        compiler_params=_cparams(("parallel", "parallel")),
        name="hyena_longconv",
    )(*args)


@functools.lru_cache(maxsize=None)
def _dft_tables(l):
    n = 2 * l
    n1, nh, h, hp = _dft_dims(l)
    m = DFT_MINOR
    n2i = np.arange(m)[:, None, None]
    k1i = np.arange(h)[None, :, None]
    n1i = np.arange(nh)[None, None, :]
    ang = 2.0 * np.pi * ((k1i * (m * n1i + n2i)) % n) / n
    tf1 = np.zeros((m, 2 * hp, nh), np.float32)
    tf1[:, :h] = np.cos(ang)
    tf1[:, hp:hp + h] = -np.sin(ang)
    w = np.full((h,), 2.0)
    w[0] = 1.0
    w[-1] = 1.0
    ti2 = np.zeros((m, nh, 2 * hp), np.float32)
    ti2[:, :, :h] = np.transpose(np.cos(ang) * (w[None, :, None] / n), (0, 2, 1))
    ti2[:, :, hp:hp + h] = np.transpose(-np.sin(ang) * (w[None, :, None] / n), (0, 2, 1))
    th = 2.0 * np.pi * ((np.arange(m)[:, None] * np.arange(m)[None, :]) % m) / m
    cm, sm = np.cos(th), np.sin(th)
    gf = np.block([[cm, sm], [-sm, cm]]).astype(np.float32)
    gi = np.block([[cm, -sm], [sm, cm]]).astype(np.float32)
    return tf1, gf, gi, ti2


@functools.lru_cache(maxsize=None)
def _filter_tables(l):
    bands = (HY_EMB - 1) // 2
    t = np.linspace(0.0, 1.0, l)[:, None]
    ang = 2.0 * np.pi * np.arange(l)[:, None] / l
    fb = np.linspace(1e-4, bands - 1, bands)[None, :]
    z = np.concatenate([t, np.cos(fb * ang), -np.sin(fb * ang)], axis=-1)
    zt = np.zeros((l, LANE), np.float32)
    zt[:, :HY_EMB] = z
    zt[:, HY_EMB] = 1.0
    zt[0, HY_EMB] = 0.0
    max_decay = math.log(HY_TARGET) / HY_FAST
    min_decay = math.log(HY_TARGET) / HY_SLOW
    deltas = np.abs(np.linspace(min_decay, max_decay, HY_WIDTH))[None, :].astype(np.float32)
    return zt, deltas


def _rope_rows():
    inv_freq = 1.0 / (ROPE_THETA ** (np.arange(0, ROPE, 2, dtype=np.float32) / ROPE))
    half = ROPE // 2
    freq = np.zeros((1, HEAD_PAD), np.float32)
    lo = np.zeros((1, HEAD_PAD), np.float32)
    hi = np.zeros((1, HEAD_PAD), np.float32)
    freq[0, NOPE:NOPE + half] = inv_freq
    freq[0, NOPE + half:NOPE + ROPE] = inv_freq
    lo[0, NOPE:NOPE + half] = -1.0
    hi[0, NOPE + half:NOPE + ROPE] = 1.0
    vone = np.zeros((1, HEAD_PAD), np.float32)
    vone[0, VDIM] = 1.0
    return freq, lo, hi, vone


def _pad_cols(w, width, at=0):
    out = jnp.zeros(w.shape[:-1] + (width,), w.dtype)
    return out.at[..., at:at + w.shape[-1]].set(w)


def _pad2(w, rows, cols):
    out = jnp.zeros((rows, cols), w.dtype)
    return out.at[:w.shape[0], :w.shape[1]].set(w)


def kernel(x, positions, ffn1_norm, ffn1_w_gate, ffn1_w_up, ffn1_w_down, mix_norm, w_in, mla_q_norm, mla_w_qb, mla_kv_norm, mla_w_kvb, conv_dw_w, conv_dw_b, conv_ln_g, conv_ln_b, hy_short_w, hy_short_b, hy_filt_w1, hy_filt_b1, hy_filt_w2, hy_filt_b2, hy_filt_w3, hy_filt_b3, hy_filt_w4, hy_filt_freq, hy_bias_d, out_norm, w_out, ffn2_norm, ffn2_w_gate, ffn2_w_up, ffn2_w_down, final_norm):
    b, l, d = x.shape
    n = b * l
    depth = w_in.shape[0]
    row = lambda v: v.reshape(1, -1).astype(F32)

    tf1, gf, gi, ti2 = (jnp.asarray(t).astype(BF16) for t in _dft_tables(l))
    zt_np, deltas_np = _filter_tables(l)
    zt, deltas = jnp.asarray(zt_np), jnp.asarray(deltas_np)
    freq, lo, hi, vone = (jnp.asarray(t) for t in _rope_rows())
    pos = positions.astype(F32).reshape(n, 1)

    f1g, f1u, f1d, f2g, f2u, f2d, wo = (_cast_call(w) for w in (
        ffn1_w_gate, ffn1_w_up, ffn1_w_down, ffn2_w_gate, ffn2_w_up, ffn2_w_down, w_out))

    xs = x.reshape(n, d)
    for i in range(depth):
        xs = _ffn_call(xs, row(ffn1_norm[i]), f1g, f1u, f1d, i)

        wi = w_in[i]
        win = jnp.concatenate([
            wi[:, OFF_Q:OFF_KV], wi[:, OFF_KV:OFF_KPE], _pad_cols(wi[:, OFF_KPE:OFF_CONV], HEAD_PAD, NOPE),
            wi[:, OFF_CONV:]], axis=1).astype(BF16)
        wq = _pad_cols(mla_w_qb[i].reshape(Q_RANK, HEADS, QK_DIM), HEAD_PAD)
        wq = wq.reshape(Q_RANK, HEADS * HEAD_PAD).astype(BF16)
        wkv = mla_w_kvb[i].reshape(KV_RANK, HEADS, NOPE + VDIM)
        wk = _pad_cols(wkv[..., :NOPE], HEAD_PAD).reshape(KV_RANK, HEADS * HEAD_PAD).astype(BF16)
        wv = _pad_cols(wkv[..., NOPE:], HEAD_PAD).reshape(KV_RANK, HEADS * HEAD_PAD).astype(BF16)

        q, k, v, loc = _proj_call(xs, pos, b, l, row(mix_norm[i]), win, row(mla_q_norm[i]), wq,
                                  row(mla_kv_norm[i]), wk, wv, freq, lo, hi, vone)
        y_mla = _attn_call(q, k, v).reshape(n, MLA_WIDTH)

        y_conv, hv, hx1, hx2 = _local_call(
            loc.reshape(b, l, LOC_COLS), conv_dw_w[i], row(conv_dw_b[i]), row(conv_ln_g[i]),
            row(conv_ln_b[i]), hy_short_w[i], row(hy_short_b[i]))

        hrow = lambda v_: _pad2(v_.reshape(1, -1), 1, LANE)
        kfwd, kbwd, ssum = _filter_call(
            zt, _pad2(hy_filt_w1[i], LANE, LANE), hrow(hy_filt_b1[i]), _pad2(hy_filt_w2[i], LANE, LANE),
            hrow(hy_filt_b2[i]), _pad2(hy_filt_w3[i], LANE, LANE), hrow(hy_filt_b3[i]),
            _pad2(hy_filt_w4[i], LANE, HY_ORDER * 2 * HY_WIDTH), hrow(hy_filt_freq[i]), deltas)
        kf = _spectrum_call(kfwd, kbwd, ssum, tf1, gf, l)
        dd = hy_bias_d[i].reshape(HY_ORDER, 1, HY_WIDTH)
        y1 = _longconv_call(hv, None, kf, dd, tf1, gf, gi, ti2, 0, l)
        y2 = _longconv_call(hx1, y1, kf, dd, tf1, gf, gi, ti2, 1, l)

        last = i == depth - 1
        xs = _ffn_call(xs, row(ffn2_norm[i]), f2g, f2u, f2d, i,
                       mix=(y_mla, y_conv.reshape(n, -1), y2.reshape(-1, HY_WIDTH), hx2.reshape(n, -1),
                            row(out_norm[i]), wo),
                       final_g=row(final_norm) if last else None)
    return xs.reshape(b, l, d)
```

```python
import functools
import math

import numpy as np
import jax
import jax.numpy as jnp
from jax import lax
from jax.experimental import pallas as pl
from jax.experimental.pallas import tpu as pltpu

F32 = jnp.float32
BF16 = jnp.bfloat16

D_MODEL = 1024
NORM_EPS = 1e-6
D_FF = 2816
HEADS = 8
NOPE = 64
ROPE = 32
VDIM = 64
QK_DIM = NOPE + ROPE
Q_RANK = 256
KV_RANK = 128
MLA_WIDTH = HEADS * VDIM
ROPE_THETA = 10000.0
CONV_WIDTH = 256
CONV_KERNEL = 31
HY_WIDTH = 256
HY_ORDER = 2
HY_SHORT = 3
HY_EMB = 33
HY_HID = 64
HY_FAST, HY_SLOW, HY_TARGET = 0.3, 1.5, 1e-2
OFF_Q = 0
OFF_KV = OFF_Q + Q_RANK
OFF_KPE = OFF_KV + KV_RANK
OFF_CONV = OFF_KPE + ROPE
OFF_HY = OFF_CONV + 2 * CONV_WIDTH
IN_COLS = OFF_HY + 3 * HY_WIDTH
LOC_COLS = 2 * CONV_WIDTH + 3 * HY_WIDTH

LANE = 128
HEAD_PAD = LANE
DFT_MINOR = 128
HALO = 16
LOCAL_ROWS = 64
DFT_UNROLL = 16
SUBLANES = 8


def _pitch(rows):
    q = -(-rows // SUBLANES)
    return (q | 1) * SUBLANES


SLAB_PITCH = _pitch(DFT_MINOR)


def _slab_rows(l):
    return (l // DFT_MINOR) * SLAB_PITCH


def _store_slabs(o_ref, lead, r0, val):
    j, off = divmod(r0, DFT_MINOR)
    base = j * SLAB_PITCH + off
    o_ref[lead + (slice(base, base + val.shape[0]), slice(None))] = val


def _zero_slab_pads(o_ref, lead, nslabs):
    pad = SLAB_PITCH - DFT_MINOR
    for j in range(nslabs):
        o_ref[lead + (slice(j * SLAB_PITCH + DFT_MINOR, (j + 1) * SLAB_PITCH), slice(None))] = (
            jnp.zeros((pad, o_ref.shape[-1]), o_ref.dtype))
VMEM_LIMIT = 56 * 1024 * 1024


def _cparams(sem):
    return pltpu.CompilerParams(dimension_semantics=sem, vmem_limit_bytes=VMEM_LIMIT)


def _const_spec(shape):
    nd = len(shape)
    return pl.BlockSpec(shape, lambda *_: (0,) * nd, pipeline_mode=pl.Buffered(1))


def _rms(x, g):
    ms = jnp.mean(x * x, axis=-1, keepdims=True)
    return x * lax.rsqrt(ms + NORM_EPS) * g


def _dot(a, b):
    return jnp.dot(a, b, preferred_element_type=F32)


def _ffn_body(*refs, mix, final, ff_chunk):
    refs = list(refs)
    x_ref = refs.pop(0)
    if mix:
        ym_ref, yc_ref, yh_ref, x2_ref, on_ref, wo_ref = refs[:6]
        refs = refs[6:]
    g_ref, wg_ref, wu_ref, wd_ref = refs[:4]
    refs = refs[4:]
    if final:
        fin_ref = refs.pop(0)
    o_ref, a_scr = refs

    x = x_ref[...]
    if mix:
        on = on_ref[...]
        e1, e2 = MLA_WIDTH, MLA_WIDTH + CONV_WIDTH
        yh = jnp.concatenate([yh_ref[j * SLAB_PITCH:j * SLAB_PITCH + DFT_MINOR, :]
                              for j in range(x.shape[0] // DFT_MINOR)], axis=0)
        y = jnp.concatenate([
            _rms(ym_ref[...], on[:, :e1]),
            _rms(yc_ref[...], on[:, e1:e2]),
            _rms(yh * x2_ref[...], on[:, e2:]),
        ], axis=-1).astype(BF16)
        x = x + _dot(y, wo_ref[0])
    h = _rms(x, g_ref[...]).astype(BF16)
    for c in range(D_FF // ff_chunk):
        sl = slice(c * ff_chunk, (c + 1) * ff_chunk)
        g = _dot(h, wg_ref[0, :, sl])
        u = _dot(h, wu_ref[0, :, sl])
        a_scr[:, sl] = (g * jax.nn.sigmoid(g) * u).astype(BF16)
    y = x + 0.5 * _dot(a_scr[...], wd_ref[0])
    if final:
        y = _rms(y, fin_ref[...])
    o_ref[...] = y


def _cast_body(w_ref, o_ref):
    o_ref[...] = w_ref[...].astype(o_ref.dtype)


def _cast_call(w):
    depth, r, c = w.shape
    rows = next(t for t in (512, 256, r) if r % t == 0)
    spec = pl.BlockSpec((1, rows, c), lambda d, i: (d, i, 0))
    return pl.pallas_call(
        _cast_body,
        grid=(depth, r // rows),
        in_specs=[spec],
        out_specs=spec,
        out_shape=jax.ShapeDtypeStruct(w.shape, BF16),
        compiler_params=_cparams(("parallel", "parallel")),
        name="weight_cast",
    )(w)


def _layer_spec(w, layer):
    return pl.BlockSpec((1,) + w.shape[1:], lambda *_: (layer, 0, 0), pipeline_mode=pl.Buffered(1))


def _ffn_call(x, norm_g, wg, wu, wd, layer, mix=None, final_g=None, tm=512, ff_chunk=256):
    n, d = x.shape
    tm = min(tm, n)
    row = lambda w: pl.BlockSpec((tm, w), lambda i: (i, 0))
    args, specs = [x], [row(d)]
    if mix is not None:
        ym, yc, yh, x2, on, wo = mix
        args += [ym, yc, yh, x2, on, wo]
        specs += [row(ym.shape[1]), row(yc.shape[1]),
                  pl.BlockSpec((_slab_rows(tm), yh.shape[1]), lambda i: (i, 0)), row(x2.shape[1]),
                  _const_spec(on.shape), _layer_spec(wo, layer)]
    args += [norm_g, wg, wu, wd]
    specs += [_const_spec(norm_g.shape), _layer_spec(wg, layer), _layer_spec(wu, layer), _layer_spec(wd, layer)]
    if final_g is not None:
        args.append(final_g)
        specs.append(_const_spec(final_g.shape))
    return pl.pallas_call(
        functools.partial(_ffn_body, mix=mix is not None, final=final_g is not None, ff_chunk=ff_chunk),
        grid=(n // tm,),
        in_specs=specs,
        out_specs=row(d),
        out_shape=jax.ShapeDtypeStruct((n, d), F32),
        scratch_shapes=[pltpu.VMEM((tm, D_FF), BF16)],
        compiler_params=_cparams(("parallel",)),
        name="ffn_mix" if mix is not None else "ffn",
    )(*args)


def _proj_body(x_ref, pos_ref, g_ref, win_ref, qn_ref, wq_ref, kvn_ref, wk_ref, wv_ref,
               freq_ref, lo_ref, hi_ref, vone_ref, q_ref, kt_ref, v_ref, loc_ref):
    xn = _rms(x_ref[...], g_ref[...]).astype(BF16)
    h = _dot(xn, win_ref[...])
    ang = pos_ref[...] * freq_ref[...]
    cos = jnp.cos(ang)
    sin = jnp.sin(ang)
    sin_lo = sin * lo_ref[...]
    sin_hi = sin * hi_ref[...]
    half = ROPE // 2

    def rope(t, c, s_lo, s_hi):
        return (t * c + pltpu.roll(t, HEAD_PAD - half, axis=1) * s_lo + pltpu.roll(t, half, axis=1) * s_hi)

    o1 = Q_RANK
    o2 = o1 + KV_RANK
    o3 = o2 + HEAD_PAD
    cq = _rms(h[:, :o1], qn_ref[...]).astype(BF16)
    ckv = _rms(h[:, o1:o2], kvn_ref[...]).astype(BF16)
    kpe = rope(h[:, o2:o3], cos, sin_lo, sin_hi)
    loc_ref[...] = h[:, o3:]
    qa = _dot(cq, wq_ref[...])
    kk = _dot(ckv, wk_ref[...])
    vv = _dot(ckv, wv_ref[...])
    scale = QK_DIM ** -0.5 * math.log2(math.e)
    cs, s_lo, s_hi = cos * scale, sin_lo * scale, sin_hi * scale
    vone = vone_ref[...]
    for hd in range(HEADS):
        sl = slice(hd * HEAD_PAD, (hd + 1) * HEAD_PAD)
        q_ref[0, hd] = rope(qa[:, sl], cs, s_lo, s_hi).astype(BF16)
        kt_ref[0, hd] = (kk[:, sl] + kpe).T.astype(BF16)
        v_ref[0, hd] = (vv[:, sl] + vone).astype(BF16)


def _proj_call(x, pos, b, l, g, win, qn, wq, kvn, wk, wv, freq, lo, hi, vone, tm=1024):
    n, d = x.shape
    tm = min(tm, l)
    nlt = l // tm
    head_spec = pl.BlockSpec((1, HEADS, tm, HEAD_PAD), lambda i: (i // nlt, 0, i % nlt, 0))
    head_shape = jax.ShapeDtypeStruct((b, HEADS, l, HEAD_PAD), BF16)
    kt_spec = pl.BlockSpec((1, HEADS, HEAD_PAD, tm), lambda i: (i // nlt, 0, 0, i % nlt))
    kt_shape = jax.ShapeDtypeStruct((b, HEADS, HEAD_PAD, l), BF16)
    consts = [g, win, qn, wq, kvn, wk, wv, freq, lo, hi, vone]
    return pl.pallas_call(
        _proj_body,
        grid=(n // tm,),
        in_specs=[pl.BlockSpec((tm, d), lambda i: (i, 0)), pl.BlockSpec((tm, 1), lambda i: (i, 0))]
        + [_const_spec(c.shape) for c in consts],
        out_specs=[head_spec, kt_spec, head_spec, pl.BlockSpec((tm, LOC_COLS), lambda i: (i, 0))],
        out_shape=[head_shape, kt_shape, head_shape, jax.ShapeDtypeStruct((n, LOC_COLS), F32)],
        compiler_params=_cparams(("parallel",)),
        name="in_proj",
    )(x, pos, *consts)


def _attn_body(q_ref, kt_ref, v_ref, o_ref, *, tk, heads_per_step):
    l = kt_ref.shape[3]
    tq = q_ref.shape[2]
    hps = heads_per_step
    qs = [q_ref[0, hd] for hd in range(hps)]

    def step(j, carry):
        off = pl.multiple_of(j * tk, tk)
        new = []
        for hd in range(hps):
            m, acc = carry[hd]
            kt = kt_ref[0, hd, :, pl.ds(off, tk)]
            v = v_ref[0, hd, pl.ds(off, tk), :]
            s = _dot(qs[hd], kt)
            m_new = jnp.maximum(m, jnp.max(s, axis=-1, keepdims=True))
            alpha = jnp.exp2(m - m_new)
            p = jnp.exp2(s - m_new).astype(BF16)
            new.append((m_new, alpha * acc + _dot(p, v)))
        return tuple(new)

    init = tuple((jnp.full((tq, 1), -1e30, F32), jnp.zeros((tq, HEAD_PAD), F32)) for _ in range(hps))
    fin = lax.fori_loop(0, l // tk, step, init, unroll=True)
    o_ref[0] = jnp.concatenate([acc[:, :VDIM] / acc[:, VDIM:VDIM + 1] for _, acc in fin], axis=-1)


def _attn_call(q, kt, v, tq=1024, tk=2048, heads_per_step=2):
    b, hn, l, dp = q.shape
    tq, tk = min(tq, l), min(tk, l)
    hps = heads_per_step
    return pl.pallas_call(
        functools.partial(_attn_body, tk=tk, heads_per_step=hps),
        grid=(b, hn // hps, l // tq),
        in_specs=[pl.BlockSpec((1, hps, tq, dp), lambda bi, hi, qi: (bi, hi, qi, 0)),
                  pl.BlockSpec((1, hps, dp, l), lambda bi, hi, qi: (bi, hi, 0, 0)),
                  pl.BlockSpec((1, hps, l, dp), lambda bi, hi, qi: (bi, hi, 0, 0))],
        out_specs=pl.BlockSpec((1, tq, hps * VDIM), lambda bi, hi, qi: (bi, qi, hi)),
        out_shape=jax.ShapeDtypeStruct((b, l, hn * VDIM), F32),
        compiler_params=_cparams(("parallel", "parallel", "arbitrary")),
        name="mla_attn",
    )(q, kt, v)


def _local_body(cur_ref, prev_ref, next_ref, cw_ref, cb_ref, lg_ref, lb_ref, sw_ref, sb_ref,
                yc_ref, hv_ref, hx1_ref, hx2_ref, u_scr, s_scr):
    t = cur_ref.shape[1]
    i = pl.program_id(1)
    first = i == 0
    last = i == pl.num_programs(1) - 1
    cw = 2 * CONV_WIDTH

    def glu(w):
        return w[:, :CONV_WIDTH] * jax.nn.sigmoid(w[:, CONV_WIDTH:cw])

    prev = prev_ref[0]
    nxt = next_ref[0]
    cur = cur_ref[0]
    pmask = jnp.where(first, 0.0, 1.0)
    nmask = jnp.where(last, 0.0, 1.0)
    u_scr[0:HALO, :] = glu(prev) * pmask
    u_scr[HALO:HALO + t, :] = glu(cur)
    u_scr[HALO + t:, :] = glu(nxt) * nmask
    s_scr[0:HALO, :] = prev[:, cw:] * pmask
    s_scr[HALO:HALO + t, :] = cur[:, cw:]
    s_scr[HALO + t:, :] = nxt[:, cw:] * nmask

    pad = CONV_KERNEL // 2
    sp = HY_SHORT // 2
    rc = min(LOCAL_ROWS, t)
    for r0 in range(0, t, rc):
        acc = jnp.zeros((rc, CONV_WIDTH), F32) + cb_ref[...]
        wins = {}
        for r in range(SUBLANES):
            part = None
            for kk in range(CONV_KERNEL):
                a, rr = divmod(HALO - pad + kk, SUBLANES)
                if rr != r:
                    continue
                if a not in wins:
                    wins[a] = u_scr[pl.ds(r0 + a * SUBLANES, rc + SUBLANES), :]
                term = wins[a] * cw_ref[kk:kk + 1, :]
                part = term if part is None else part + term
            if part is not None:
                acc = acc + part[r:r + rc]
        mu = jnp.mean(acc, axis=-1, keepdims=True)
        cen = acc - mu
        var = jnp.mean(cen * cen, axis=-1, keepdims=True)
        yn = cen * lax.rsqrt(var + NORM_EPS) * lg_ref[...] + lb_ref[...]
        yc_ref[0, r0:r0 + rc, :] = yn * jax.nn.sigmoid(yn)

        for gi, o_ref in enumerate((hv_ref, hx1_ref, hx2_ref)):
            cs = slice(gi * HY_WIDTH, (gi + 1) * HY_WIDTH)
            hs = jnp.zeros((rc, HY_WIDTH), F32) + sb_ref[:, cs]
            for kk in range(HY_SHORT):
                hs = hs + s_scr[pl.ds(r0 + HALO - sp + kk, rc), cs] * sw_ref[kk:kk + 1, cs]
            if o_ref is hx2_ref:
                o_ref[0, r0:r0 + rc, :] = hs
            else:
                _store_slabs(o_ref, (0,), r0, hs)
    for o_ref in (hv_ref, hx1_ref):
        _zero_slab_pads(o_ref, (0,), t // DFT_MINOR)


def _local_call(loc, cw, cb, lg, lb, sw, sb, t=512):
    b, l, c = loc.shape
    t = min(t, l)
    r = t // HALO
    nh = l // HALO
    consts = [cw, cb, lg, lb, sw, sb]
    out_spec = pl.BlockSpec((1, t, HY_WIDTH), lambda bi, i: (bi, i, 0))
    out_shape = jax.ShapeDtypeStruct((b, l, HY_WIDTH), F32)
    slab_spec = pl.BlockSpec((1, _slab_rows(t), HY_WIDTH), lambda bi, i: (bi, i, 0))
    slab_shape = jax.ShapeDtypeStruct((b, _slab_rows(l), HY_WIDTH), F32)
    return pl.pallas_call(
        _local_body,
        grid=(b, l // t),
        in_specs=[pl.BlockSpec((1, t, c), lambda bi, i: (bi, i, 0)),
                  pl.BlockSpec((1, HALO, c), lambda bi, i: (bi, jnp.maximum(i * r - 1, 0), 0)),
                  pl.BlockSpec((1, HALO, c), lambda bi, i: (bi, jnp.minimum((i + 1) * r, nh - 1), 0))]
        + [_const_spec(a.shape) for a in consts],
        out_specs=[out_spec, slab_spec, slab_spec, out_spec],
        out_shape=[out_shape, slab_shape, slab_shape, out_shape],
        scratch_shapes=[pltpu.VMEM((t + 2 * HALO, CONV_WIDTH), F32),
                        pltpu.VMEM((t + 2 * HALO, 3 * HY_WIDTH), F32)],
        compiler_params=_cparams(("parallel", "parallel")),
        name="local_mixers",
    )(loc, loc, loc, *consts)


def _filter_body(z_ref, w1_ref, b1_ref, w2_ref, b2_ref, w3_ref, b3_ref, w4_ref, fr_ref, dl_ref,
                 kf_ref, kb_ref, s_ref):
    hp = lax.Precision.HIGHEST
    z = z_ref[...]
    fr = fr_ref[...]
    hd = jnp.sin(fr * (jnp.dot(z, w1_ref[...], precision=hp, preferred_element_type=F32) + b1_ref[...]))
    hd = jnp.sin(fr * (jnp.dot(hd, w2_ref[...], precision=hp, preferred_element_type=F32) + b2_ref[...]))
    hd = jnp.sin(fr * (jnp.dot(hd, w3_ref[...], precision=hp, preferred_element_type=F32) + b3_ref[...]))
    h = jnp.dot(hd, w4_ref[...], precision=hp, preferred_element_type=F32)
    tcol = z[:, 0:1]
    win = jnp.exp(-tcol * dl_ref[...])
    bwin = win * z[:, HY_EMB:HY_EMB + 1]

    @pl.when(pl.program_id(0) == 0)
    def _():
        s_ref[...] = jnp.zeros_like(s_ref)

    for o in range(HY_ORDER):
        asum = None
        for side, (k_ref, w) in enumerate(((kf_ref, win), (kb_ref, bwin))):
            c0 = (o * 2 + side) * HY_WIDTH
            ko = h[:, c0:c0 + HY_WIDTH] * w
            for r0 in range(0, ko.shape[0], DFT_MINOR):
                _store_slabs(k_ref, (o,), r0, ko[r0:r0 + DFT_MINOR])
            _zero_slab_pads(k_ref, (o,), ko.shape[0] // DFT_MINOR)
            part = jnp.sum(jnp.abs(ko), axis=0, keepdims=True)
            asum = part if asum is None else asum + part
        s_ref[o] += jnp.broadcast_to(asum, s_ref.shape[1:])


def _filter_call(zt, w1, b1, w2, b2, w3, b3, w4, fr, dl, t=512):
    l = zt.shape[0]
    t = min(t, l)
    consts = [w1, b1, w2, b2, w3, b3, w4, fr, dl]
    k_spec = pl.BlockSpec((HY_ORDER, _slab_rows(t), HY_WIDTH), lambda i: (0, i, 0))
    k_shape = jax.ShapeDtypeStruct((HY_ORDER, _slab_rows(l), HY_WIDTH), F32)
    return pl.pallas_call(
        _filter_body,
        grid=(l // t,),
        in_specs=[pl.BlockSpec((t, zt.shape[1]), lambda i: (i, 0))] + [_const_spec(a.shape) for a in consts],
        out_specs=[k_spec, k_spec, pl.BlockSpec((HY_ORDER, 8, HY_WIDTH), lambda i: (0, 0, 0))],
        out_shape=[k_shape, k_shape, jax.ShapeDtypeStruct((HY_ORDER, 8, HY_WIDTH), F32)],
        compiler_params=_cparams(("arbitrary",)),
        name="hyena_filter",
    )(zt, *consts)


def _dft_dims(l):
    n1 = 2 * l // DFT_MINOR
    nh = n1 // 2
    h = nh + 1
    hp = -(-h // 8) * 8
    return n1, nh, h, hp


def _unroll(trips):
    return max(u for u in range(1, DFT_UNROLL + 1) if trips % u == 0)


def _fwd_stage1(load_slab, tf1_ref, x1_scr, hp):
    def body(n2, c):
        u = load_slab(n2).astype(BF16)
        x1 = _dot(tf1_ref[n2], u)
        x1_scr[pl.ds(pl.multiple_of(n2 * _pitch(2 * hp), SUBLANES), 2 * hp), :] = x1
        return c
    lax.fori_loop(0, DFT_MINOR, body, 0, unroll=_unroll(DFT_MINOR))


def _stage2_operand(k1, x1_scr, hp):
    are = x1_scr[pl.ds(k1, DFT_MINOR, stride=_pitch(2 * hp)), :]
    aim = x1_scr[pl.ds(hp + k1, DFT_MINOR, stride=_pitch(2 * hp)), :]
    return jnp.concatenate([are, aim], axis=0).astype(BF16)


def _spectrum_body(kf_ref, kb_ref, s_ref, tf1_ref, gf_ref, o_ref, x1_scr, *, l):
    _, nh, h, hp = _dft_dims(l)
    inv = 1.0 / s_ref[0, 0:1, :]

    for half, ref in enumerate((kf_ref, kb_ref)):
        _fwd_stage1(lambda n2, ref=ref: ref[0, pl.ds(n2, nh, stride=SLAB_PITCH), :], tf1_ref, x1_scr, hp)

        def body(j, c, half=half):
            ks = (2 * j, 2 * j + 1)
            xx = _dot(gf_ref[...], jnp.concatenate([_stage2_operand(k1, x1_scr, hp) for k1 in ks], axis=1))
            m = DFT_MINOR
            cb = o_ref.shape[3]
            for i, k1 in enumerate(ks):
                x = xx[:, i * cb:(i + 1) * cb] * inv
                if half == 0:
                    o_ref[0, k1] = x
                else:
                    o_ref[0, k1, 0:m, :] = o_ref[0, k1, 0:m, :] + x[:m]
                    o_ref[0, k1, m:, :] = o_ref[0, k1, m:, :] - x[m:]
            return c
        lax.fori_loop(0, hp // 2, body, 0, unroll=_unroll(hp // 4))


def _spectrum_call(kfwd, kbwd, ssum, tf1, gf, l):
    _, nh, h, hp = _dft_dims(l)
    order, _, c = kfwd.shape
    cb = LANE
    return pl.pallas_call(
        functools.partial(_spectrum_body, l=l),
        grid=(order, c // cb),
        in_specs=[pl.BlockSpec((1, _slab_rows(l), cb), lambda o, ci: (o, 0, ci), pipeline_mode=pl.Buffered(1)),
                  pl.BlockSpec((1, _slab_rows(l), cb), lambda o, ci: (o, 0, ci), pipeline_mode=pl.Buffered(1)),
                  pl.BlockSpec((1, 8, cb), lambda o, ci: (o, 0, ci)),
                  _const_spec(tf1.shape), _const_spec(gf.shape)],
        out_specs=pl.BlockSpec((1, hp, 2 * DFT_MINOR, cb), lambda o, ci: (o, 0, 0, ci),
                               pipeline_mode=pl.Buffered(1)),
        out_shape=jax.ShapeDtypeStruct((order, hp, 2 * DFT_MINOR, c), F32),
        scratch_shapes=[pltpu.VMEM((DFT_MINOR * _pitch(2 * hp), cb), F32)],
        compiler_params=_cparams(("parallel", "parallel")),
        name="hyena_spectrum",
    )(kfwd, kbwd, ssum, tf1, gf)


def _longconv_body(*refs, gated, l):
    if gated:
        a_ref, b_ref, kf_ref, d_ref, tf1_ref, gf_ref, gi_ref, ti2_ref, o_ref, x1_scr, z_scr = refs
    else:
        a_ref, kf_ref, d_ref, tf1_ref, gf_ref, gi_ref, ti2_ref, o_ref, x1_scr, z_scr = refs
        b_ref = None
    _, nh, h, hp = _dft_dims(l)
    m = DFT_MINOR

    zp = _pitch(2 * m)

    def load_slab(n2):
        u = a_ref[0, pl.ds(n2, nh, stride=SLAB_PITCH), :]
        if gated:
            u = u * b_ref[0, pl.ds(n2, nh, stride=SLAB_PITCH), :]
        return u

    _fwd_stage1(load_slab, tf1_ref, x1_scr, hp)

    cb = o_ref.shape[2]

    def mid(j, c):
        ks = (2 * j, 2 * j + 1)
        x = _dot(gf_ref[...], jnp.concatenate([_stage2_operand(k1, x1_scr, hp) for k1 in ks], axis=1))
        xre, xim = x[:m], x[m:]
        kre = jnp.concatenate([kf_ref[0, k1, 0:m, :] for k1 in ks], axis=1)
        kim = jnp.concatenate([kf_ref[0, k1, m:2 * m, :] for k1 in ks], axis=1)
        y = jnp.concatenate([xre * kre - xim * kim, xre * kim + xim * kre], axis=0).astype(BF16)
        z = _dot(gi_ref[...], y)
        for i, k1 in enumerate(ks):
            z_scr[pl.ds(pl.multiple_of(k1 * zp, SUBLANES), 2 * m), :] = z[:, i * cb:(i + 1) * cb]
        return c
    lax.fori_loop(0, hp // 2, mid, 0, unroll=_unroll(hp // 2))

    def last(n2, c):
        zre = z_scr[pl.ds(n2, hp, stride=zp), :]
        zim = z_scr[pl.ds(m + n2, hp, stride=zp), :]
        rhs = jnp.concatenate([zre, zim], axis=0).astype(BF16)
        y = _dot(ti2_ref[n2], rhs)
        o_ref[0, pl.ds(n2, nh, stride=SLAB_PITCH), :] = y
        return c
    lax.fori_loop(0, m, last, 0, unroll=_unroll(m))

    def epi(j, c):
        sl = pl.ds(pl.multiple_of(j * SLAB_PITCH, SUBLANES), m)
        u = a_ref[0, sl, :]
        if gated:
            u = u * b_ref[0, sl, :]
        o_ref[0, sl, :] = o_ref[0, sl, :] + u * d_ref[0]
        pad = pl.ds(pl.multiple_of(j * SLAB_PITCH + m, SUBLANES), SLAB_PITCH - m)
        o_ref[0, pad, :] = jnp.zeros((SLAB_PITCH - m, o_ref.shape[2]), F32)
        return c
    lax.fori_loop(0, nh, epi, 0, unroll=_unroll(nh))


def _longconv_call(a, b, kf, d, tf1, gf, gi, ti2, order, l):
    bn, _, c = a.shape
    _, nh, h, hp = _dft_dims(l)
    cb = LANE
    sig = pl.BlockSpec((1, _slab_rows(l), cb), lambda ci, bi: (bi, 0, ci), pipeline_mode=pl.Buffered(1))
    args = [a] + ([b] if b is not None else [])
    specs = [sig] * len(args)
    args += [kf, d, tf1, gf, gi, ti2]
    specs += [pl.BlockSpec((1, hp, 2 * DFT_MINOR, cb), lambda ci, bi: (order, 0, 0, ci),
                           pipeline_mode=pl.Buffered(1)),
              pl.BlockSpec((1, 1, cb), lambda ci, bi: (order, 0, ci)),
              _const_spec(tf1.shape), _const_spec(gf.shape), _const_spec(gi.shape), _const_spec(ti2.shape)]
    return pl.pallas_call(
        functools.partial(_longconv_body, gated=b is not None, l=l),
        grid=(c // cb, bn),
        in_specs=specs,
        out_specs=pl.BlockSpec((1, _slab_rows(l), cb), lambda ci, bi: (bi, 0, ci)),
        out_shape=jax.ShapeDtypeStruct((bn, _slab_rows(l), c), F32),
        scratch_shapes=[pltpu.VMEM((DFT_MINOR * _pitch(2 * hp), cb), F32),
                        pltpu.VMEM((hp * _pitch(2 * DFT_MINOR), cb), F32)],
        compiler_params=_cparams(("parallel", "parallel")),
        name="hyena_longconv",
    )(*args)


@functools.lru_cache(maxsize=None)
def _dft_tables(l):
    n = 2 * l
    n1, nh, h, hp = _dft_dims(l)
    m = DFT_MINOR
    n2i = np.arange(m)[:, None, None]
    k1i = np.arange(h)[None, :, None]
    n1i = np.arange(nh)[None, None, :]
    ang = 2.0 * np.pi * ((k1i * (m * n1i + n2i)) % n) / n
    tf1 = np.zeros((m, 2 * hp, nh), np.float32)
    tf1[:, :h] = np.cos(ang)
    tf1[:, hp:hp + h] = -np.sin(ang)
    w = np.full((h,), 2.0)
    w[0] = 1.0
    w[-1] = 1.0
    ti2 = np.zeros((m, nh, 2 * hp), np.float32)
    ti2[:, :, :h] = np.transpose(np.cos(ang) * (w[None, :, None] / n), (0, 2, 1))
    ti2[:, :, hp:hp + h] = np.transpose(-np.sin(ang) * (w[None, :, None] / n), (0, 2, 1))
    th = 2.0 * np.pi * ((np.arange(m)[:, None] * np.arange(m)[None, :]) % m) / m
    cm, sm = np.cos(th), np.sin(th)
    gf = np.block([[cm, sm], [-sm, cm]]).astype(np.float32)
    gi = np.block([[cm, -sm], [sm, cm]]).astype(np.float32)
    return tf1, gf, gi, ti2


@functools.lru_cache(maxsize=None)
def _filter_tables(l):
    bands = (HY_EMB - 1) // 2
    t = np.linspace(0.0, 1.0, l)[:, None]
    ang = 2.0 * np.pi * np.arange(l)[:, None] / l
    fb = np.linspace(1e-4, bands - 1, bands)[None, :]
    z = np.concatenate([t, np.cos(fb * ang), -np.sin(fb * ang)], axis=-1)
    zt = np.zeros((l, LANE), np.float32)
    zt[:, :HY_EMB] = z
    zt[:, HY_EMB] = 1.0
    zt[0, HY_EMB] = 0.0
    max_decay = math.log(HY_TARGET) / HY_FAST
    min_decay = math.log(HY_TARGET) / HY_SLOW
    deltas = np.abs(np.linspace(min_decay, max_decay, HY_WIDTH))[None, :].astype(np.float32)
    return zt, deltas


def _rope_rows():
    inv_freq = 1.0 / (ROPE_THETA ** (np.arange(0, ROPE, 2, dtype=np.float32) / ROPE))
    half = ROPE // 2
    freq = np.zeros((1, HEAD_PAD), np.float32)
    lo = np.zeros((1, HEAD_PAD), np.float32)
    hi = np.zeros((1, HEAD_PAD), np.float32)
    freq[0, NOPE:NOPE + half] = inv_freq
    freq[0, NOPE + half:NOPE + ROPE] = inv_freq
    lo[0, NOPE:NOPE + half] = -1.0
    hi[0, NOPE + half:NOPE + ROPE] = 1.0
    vone = np.zeros((1, HEAD_PAD), np.float32)
    vone[0, VDIM] = 1.0
    return freq, lo, hi, vone


def _pad_cols(w, width, at=0):
    out = jnp.zeros(w.shape[:-1] + (width,), w.dtype)
    return out.at[..., at:at + w.shape[-1]].set(w)


def _pad2(w, rows, cols):
    out = jnp.zeros((rows, cols), w.dtype)
    return out.at[:w.shape[0], :w.shape[1]].set(w)


def kernel(x, positions, ffn1_norm, ffn1_w_gate, ffn1_w_up, ffn1_w_down, mix_norm, w_in, mla_q_norm, mla_w_qb, mla_kv_norm, mla_w_kvb, conv_dw_w, conv_dw_b, conv_ln_g, conv_ln_b, hy_short_w, hy_short_b, hy_filt_w1, hy_filt_b1, hy_filt_w2, hy_filt_b2, hy_filt_w3, hy_filt_b3, hy_filt_w4, hy_filt_freq, hy_bias_d, out_norm, w_out, ffn2_norm, ffn2_w_gate, ffn2_w_up, ffn2_w_down, final_norm):
    b, l, d = x.shape
    n = b * l
    depth = w_in.shape[0]
    row = lambda v: v.reshape(1, -1).astype(F32)

    tf1, gf, gi, ti2 = (jnp.asarray(t).astype(BF16) for t in _dft_tables(l))
    zt_np, deltas_np = _filter_tables(l)
    zt, deltas = jnp.asarray(zt_np), jnp.asarray(deltas_np)
    freq, lo, hi, vone = (jnp.asarray(t) for t in _rope_rows())
    pos = positions.astype(F32).reshape(n, 1)

    f1g, f1u, f1d, f2g, f2u, f2d, wo = (_cast_call(w) for w in (
        ffn1_w_gate, ffn1_w_up, ffn1_w_down, ffn2_w_gate, ffn2_w_up, ffn2_w_down, w_out))

    xs = x.reshape(n, d)
    for i in range(depth):
        xs = _ffn_call(xs, row(ffn1_norm[i]), f1g, f1u, f1d, i)

        wi = w_in[i]
        win = jnp.concatenate([
            wi[:, OFF_Q:OFF_KV], wi[:, OFF_KV:OFF_KPE], _pad_cols(wi[:, OFF_KPE:OFF_CONV], HEAD_PAD, NOPE),
            wi[:, OFF_CONV:]], axis=1).astype(BF16)
        wq = _pad_cols(mla_w_qb[i].reshape(Q_RANK, HEADS, QK_DIM), HEAD_PAD)
        wq = wq.reshape(Q_RANK, HEADS * HEAD_PAD).astype(BF16)
        wkv = mla_w_kvb[i].reshape(KV_RANK, HEADS, NOPE + VDIM)
        wk = _pad_cols(wkv[..., :NOPE], HEAD_PAD).reshape(KV_RANK, HEADS * HEAD_PAD).astype(BF16)
        wv = _pad_cols(wkv[..., NOPE:], HEAD_PAD).reshape(KV_RANK, HEADS * HEAD_PAD).astype(BF16)

        q, k, v, loc = _proj_call(xs, pos, b, l, row(mix_norm[i]), win, row(mla_q_norm[i]), wq,
                                  row(mla_kv_norm[i]), wk, wv, freq, lo, hi, vone)
        y_mla = _attn_call(q, k, v).reshape(n, MLA_WIDTH)

        y_conv, hv, hx1, hx2 = _local_call(
            loc.reshape(b, l, LOC_COLS), conv_dw_w[i], row(conv_dw_b[i]), row(conv_ln_g[i]),
            row(conv_ln_b[i]), hy_short_w[i], row(hy_short_b[i]))

        hrow = lambda v_: _pad2(v_.reshape(1, -1), 1, LANE)
        kfwd, kbwd, ssum = _filter_call(
            zt, _pad2(hy_filt_w1[i], LANE, LANE), hrow(hy_filt_b1[i]), _pad2(hy_filt_w2[i], LANE, LANE),
            hrow(hy_filt_b2[i]), _pad2(hy_filt_w3[i], LANE, LANE), hrow(hy_filt_b3[i]),
            _pad2(hy_filt_w4[i], LANE, HY_ORDER * 2 * HY_WIDTH), hrow(hy_filt_freq[i]), deltas)
        kf = _spectrum_call(kfwd, kbwd, ssum, tf1, gf, l)
        dd = hy_bias_d[i].reshape(HY_ORDER, 1, HY_WIDTH)
        y1 = _longconv_call(hv, None, kf, dd, tf1, gf, gi, ti2, 0, l)
        y2 = _longconv_call(hx1, y1, kf, dd, tf1, gf, gi, ti2, 1, l)

        last = i == depth - 1
        xs = _ffn_call(xs, row(ffn2_norm[i]), f2g, f2u, f2d, i,
                       mix=(y_mla, y_conv.reshape(n, -1), y2.reshape(-1, HY_WIDTH), hx2.reshape(n, -1),
                            row(out_norm[i]), wo),
                       final_g=row(final_norm) if last else None)
    return xs.reshape(b, l, d)
```

```python
import functools
import math

import numpy as np
import jax
import jax.numpy as jnp
from jax import lax
from jax.experimental import pallas as pl
from jax.experimental.pallas import tpu as pltpu

F32 = jnp.float32
BF16 = jnp.bfloat16

D_MODEL = 1024
NORM_EPS = 1e-6
D_FF = 2816
HEADS = 8
NOPE = 64
ROPE = 32
VDIM = 64
QK_DIM = NOPE + ROPE
Q_RANK = 256
KV_RANK = 128
MLA_WIDTH = HEADS * VDIM
ROPE_THETA = 10000.0
CONV_WIDTH = 256
CONV_KERNEL = 31
HY_WIDTH = 256
HY_ORDER = 2
HY_SHORT = 3
HY_EMB = 33
HY_HID = 64
HY_FAST, HY_SLOW, HY_TARGET = 0.3, 1.5, 1e-2
OFF_Q = 0
OFF_KV = OFF_Q + Q_RANK
OFF_KPE = OFF_KV + KV_RANK
OFF_CONV = OFF_KPE + ROPE
OFF_HY = OFF_CONV + 2 * CONV_WIDTH
IN_COLS = OFF_HY + 3 * HY_WIDTH
LOC_COLS = 2 * CONV_WIDTH + 3 * HY_WIDTH

LANE = 128
HEAD_PAD = LANE
DFT_MINOR = 128
HALO = 16
LOCAL_ROWS = 64
DFT_UNROLL = 16
SUBLANES = 8


def _pitch(rows):
    q = -(-rows // SUBLANES)
    return (q | 1) * SUBLANES


SLAB_PITCH = _pitch(DFT_MINOR)


def _slab_rows(l):
    return (l // DFT_MINOR) * SLAB_PITCH


def _store_slabs(o_ref, lead, r0, val):
    j, off = divmod(r0, DFT_MINOR)
    base = j * SLAB_PITCH + off
    o_ref[lead + (slice(base, base + val.shape[0]), slice(None))] = val


def _zero_slab_pads(o_ref, lead, nslabs):
    pad = SLAB_PITCH - DFT_MINOR
    for j in range(nslabs):
        o_ref[lead + (slice(j * SLAB_PITCH + DFT_MINOR, (j + 1) * SLAB_PITCH), slice(None))] = (
            jnp.zeros((pad, o_ref.shape[-1]), o_ref.dtype))
VMEM_LIMIT = 56 * 1024 * 1024


def _cparams(sem):
    return pltpu.CompilerParams(dimension_semantics=sem, vmem_limit_bytes=VMEM_LIMIT)


def _const_spec(shape):
    nd = len(shape)
    return pl.BlockSpec(shape, lambda *_: (0,) * nd, pipeline_mode=pl.Buffered(1))


def _rms(x, g):
    ms = jnp.mean(x * x, axis=-1, keepdims=True)
    return x * lax.rsqrt(ms + NORM_EPS) * g


def _dot(a, b):
    return jnp.dot(a, b, preferred_element_type=F32)


def _ffn_body(*refs, mix, final, ff_chunk):
    refs = list(refs)
    x_ref = refs.pop(0)
    if mix:
        ym_ref, yc_ref, yh_ref, x2_ref, on_ref, wo_ref = refs[:6]
        refs = refs[6:]
    g_ref, wg_ref, wu_ref, wd_ref = refs[:4]
    refs = refs[4:]
    if final:
        fin_ref = refs.pop(0)
    o_ref, a_scr = refs

    x = x_ref[...]
    if mix:
        on = on_ref[...]
        e1, e2 = MLA_WIDTH, MLA_WIDTH + CONV_WIDTH
        yh = jnp.concatenate([yh_ref[j * SLAB_PITCH:j * SLAB_PITCH + DFT_MINOR, :]
                              for j in range(x.shape[0] // DFT_MINOR)], axis=0)
        y = jnp.concatenate([
            _rms(ym_ref[...], on[:, :e1]),
            _rms(yc_ref[...], on[:, e1:e2]),
            _rms(yh * x2_ref[...], on[:, e2:]),
        ], axis=-1).astype(BF16)
        x = x + _dot(y, wo_ref[0])
    h = _rms(x, g_ref[...]).astype(BF16)
    for c in range(D_FF // ff_chunk):
        sl = slice(c * ff_chunk, (c + 1) * ff_chunk)
        g = _dot(h, wg_ref[0, :, sl])
        u = _dot(h, wu_ref[0, :, sl])
        a_scr[:, sl] = (g * jax.nn.sigmoid(g) * u).astype(BF16)
    y = x + 0.5 * _dot(a_scr[...], wd_ref[0])
    if final:
        y = _rms(y, fin_ref[...])
    o_ref[...] = y


def _cast_body(w_ref, o_ref):
    o_ref[...] = w_ref[...].astype(o_ref.dtype)


def _cast_call(w):
    depth, r, c = w.shape
    rows = next(t for t in (512, 256, r) if r % t == 0)
    spec = pl.BlockSpec((1, rows, c), lambda d, i: (d, i, 0))
    return pl.pallas_call(
        _cast_body,
        grid=(depth, r // rows),
        in_specs=[spec],
        out_specs=spec,
        out_shape=jax.ShapeDtypeStruct(w.shape, BF16),
        compiler_params=_cparams(("parallel", "parallel")),
        name="weight_cast",
    )(w)


def _layer_spec(w, layer):
    return pl.BlockSpec((1,) + w.shape[1:], lambda *_: (layer, 0, 0), pipeline_mode=pl.Buffered(1))


def _ffn_call(x, norm_g, wg, wu, wd, layer, mix=None, final_g=None, tm=1024, ff_chunk=256):
    n, d = x.shape
    tm = min(tm, n)
    row = lambda w: pl.BlockSpec((tm, w), lambda i: (i, 0))
    args, specs = [x], [row(d)]
    if mix is not None:
        ym, yc, yh, x2, on, wo = mix
        args += [ym, yc, yh, x2, on, wo]
        specs += [row(ym.shape[1]), row(yc.shape[1]),
                  pl.BlockSpec((_slab_rows(tm), yh.shape[1]), lambda i: (i, 0)), row(x2.shape[1]),
                  _const_spec(on.shape), _layer_spec(wo, layer)]
    args += [norm_g, wg, wu, wd]
    specs += [_const_spec(norm_g.shape), _layer_spec(wg, layer), _layer_spec(wu, layer), _layer_spec(wd, layer)]
    if final_g is not None:
        args.append(final_g)
        specs.append(_const_spec(final_g.shape))
    return pl.pallas_call(
        functools.partial(_ffn_body, mix=mix is not None, final=final_g is not None, ff_chunk=ff_chunk),
        grid=(n // tm,),
        in_specs=specs,
        out_specs=row(d),
        out_shape=jax.ShapeDtypeStruct((n, d), F32),
        scratch_shapes=[pltpu.VMEM((tm, D_FF), BF16)],
        compiler_params=_cparams(("parallel",)),
        name="ffn_mix" if mix is not None else "ffn",
    )(*args)


def _proj_body(x_ref, pos_ref, g_ref, win_ref, qn_ref, wq_ref, kvn_ref, wk_ref, wv_ref,
               freq_ref, lo_ref, hi_ref, vone_ref, q_ref, kt_ref, v_ref, loc_ref):
    xn = _rms(x_ref[...], g_ref[...]).astype(BF16)
    h = _dot(xn, win_ref[...])
    ang = pos_ref[...] * freq_ref[...]
    cos = jnp.cos(ang)
    sin = jnp.sin(ang)
    sin_lo = sin * lo_ref[...]
    sin_hi = sin * hi_ref[...]
    half = ROPE // 2

    def rope(t, c, s_lo, s_hi):
        return (t * c + pltpu.roll(t, HEAD_PAD - half, axis=1) * s_lo + pltpu.roll(t, half, axis=1) * s_hi)

    o1 = Q_RANK
    o2 = o1 + KV_RANK
    o3 = o2 + HEAD_PAD
    cq = _rms(h[:, :o1], qn_ref[...]).astype(BF16)
    ckv = _rms(h[:, o1:o2], kvn_ref[...]).astype(BF16)
    kpe = rope(h[:, o2:o3], cos, sin_lo, sin_hi)
    loc_ref[...] = h[:, o3:]
    qa = _dot(cq, wq_ref[...])
    kk = _dot(ckv, wk_ref[...])
    vv = _dot(ckv, wv_ref[...])
    scale = QK_DIM ** -0.5 * math.log2(math.e)
    cs, s_lo, s_hi = cos * scale, sin_lo * scale, sin_hi * scale
    vone = vone_ref[...]
    for hd in range(HEADS):
        sl = slice(hd * HEAD_PAD, (hd + 1) * HEAD_PAD)
        q_ref[0, hd] = rope(qa[:, sl], cs, s_lo, s_hi).astype(BF16)
        kt_ref[0, hd] = (kk[:, sl] + kpe).T.astype(BF16)
        v_ref[0, hd] = (vv[:, sl] + vone).astype(BF16)


def _proj_call(x, pos, b, l, g, win, qn, wq, kvn, wk, wv, freq, lo, hi, vone, tm=1024):
    n, d = x.shape
    tm = min(tm, l)
    nlt = l // tm
    head_spec = pl.BlockSpec((1, HEADS, tm, HEAD_PAD), lambda i: (i // nlt, 0, i % nlt, 0))
    head_shape = jax.ShapeDtypeStruct((b, HEADS, l, HEAD_PAD), BF16)
    kt_spec = pl.BlockSpec((1, HEADS, HEAD_PAD, tm), lambda i: (i // nlt, 0, 0, i % nlt))
    kt_shape = jax.ShapeDtypeStruct((b, HEADS, HEAD_PAD, l), BF16)
    consts = [g, win, qn, wq, kvn, wk, wv, freq, lo, hi, vone]
    return pl.pallas_call(
        _proj_body,
        grid=(n // tm,),
        in_specs=[pl.BlockSpec((tm, d), lambda i: (i, 0)), pl.BlockSpec((tm, 1), lambda i: (i, 0))]
        + [_const_spec(c.shape) for c in consts],
        out_specs=[head_spec, kt_spec, head_spec, pl.BlockSpec((tm, LOC_COLS), lambda i: (i, 0))],
        out_shape=[head_shape, kt_shape, head_shape, jax.ShapeDtypeStruct((n, LOC_COLS), F32)],
        compiler_params=_cparams(("parallel",)),
        name="in_proj",
    )(x, pos, *consts)


def _attn_body(q_ref, kt_ref, v_ref, o_ref, *, tk, heads_per_step):
    l = kt_ref.shape[3]
    tq = q_ref.shape[2]
    hps = heads_per_step
    qs = [q_ref[0, hd] for hd in range(hps)]

    def step(j, carry):
        off = pl.multiple_of(j * tk, tk)
        new = []
        for hd in range(hps):
            m, acc = carry[hd]
            kt = kt_ref[0, hd, :, pl.ds(off, tk)]
            v = v_ref[0, hd, pl.ds(off, tk), :]
            s = _dot(qs[hd], kt)
            m_new = jnp.maximum(m, jnp.max(s, axis=-1, keepdims=True))
            alpha = jnp.exp2(m - m_new)
            p = jnp.exp2(s - m_new).astype(BF16)
            new.append((m_new, alpha * acc + _dot(p, v)))
        return tuple(new)

    init = tuple((jnp.full((tq, 1), -1e30, F32), jnp.zeros((tq, HEAD_PAD), F32)) for _ in range(hps))
    fin = lax.fori_loop(0, l // tk, step, init, unroll=True)
    o_ref[0] = jnp.concatenate([acc[:, :VDIM] / acc[:, VDIM:VDIM + 1] for _, acc in fin], axis=-1)


def _attn_call(q, kt, v, tq=1024, tk=2048, heads_per_step=2):
    b, hn, l, dp = q.shape
    tq, tk = min(tq, l), min(tk, l)
    hps = heads_per_step
    return pl.pallas_call(
        functools.partial(_attn_body, tk=tk, heads_per_step=hps),
        grid=(b, hn // hps, l // tq),
        in_specs=[pl.BlockSpec((1, hps, tq, dp), lambda bi, hi, qi: (bi, hi, qi, 0)),
                  pl.BlockSpec((1, hps, dp, l), lambda bi, hi, qi: (bi, hi, 0, 0)),
                  pl.BlockSpec((1, hps, l, dp), lambda bi, hi, qi: (bi, hi, 0, 0))],
        out_specs=pl.BlockSpec((1, tq, hps * VDIM), lambda bi, hi, qi: (bi, qi, hi)),
        out_shape=jax.ShapeDtypeStruct((b, l, hn * VDIM), F32),
        compiler_params=_cparams(("parallel", "parallel", "arbitrary")),
        name="mla_attn",
    )(q, kt, v)


def _local_body(cur_ref, prev_ref, next_ref, cw_ref, cb_ref, lg_ref, lb_ref, sw_ref, sb_ref,
                yc_ref, hv_ref, hx1_ref, hx2_ref, u_scr, s_scr):
    t = cur_ref.shape[1]
    i = pl.program_id(1)
    first = i == 0
    last = i == pl.num_programs(1) - 1
    cw = 2 * CONV_WIDTH

    def glu(w):
        return w[:, :CONV_WIDTH] * jax.nn.sigmoid(w[:, CONV_WIDTH:cw])

    prev = prev_ref[0]
    nxt = next_ref[0]
    cur = cur_ref[0]
    pmask = jnp.where(first, 0.0, 1.0)
    nmask = jnp.where(last, 0.0, 1.0)
    u_scr[0:HALO, :] = glu(prev) * pmask
    u_scr[HALO:HALO + t, :] = glu(cur)
    u_scr[HALO + t:, :] = glu(nxt) * nmask
    s_scr[0:HALO, :] = prev[:, cw:] * pmask
    s_scr[HALO:HALO + t, :] = cur[:, cw:]
    s_scr[HALO + t:, :] = nxt[:, cw:] * nmask

    pad = CONV_KERNEL // 2
    sp = HY_SHORT // 2
    rc = min(LOCAL_ROWS, t)
    for r0 in range(0, t, rc):
        acc = jnp.zeros((rc, CONV_WIDTH), F32) + cb_ref[...]
        wins = {}
        for r in range(SUBLANES):
            part = None
            for kk in range(CONV_KERNEL):
                a, rr = divmod(HALO - pad + kk, SUBLANES)
                if rr != r:
                    continue
                if a not in wins:
                    wins[a] = u_scr[pl.ds(r0 + a * SUBLANES, rc + SUBLANES), :]
                term = wins[a] * cw_ref[kk:kk + 1, :]
                part = term if part is None else part + term
            if part is not None:
                acc = acc + part[r:r + rc]
        mu = jnp.mean(acc, axis=-1, keepdims=True)
        cen = acc - mu
        var = jnp.mean(cen * cen, axis=-1, keepdims=True)
        yn = cen * lax.rsqrt(var + NORM_EPS) * lg_ref[...] + lb_ref[...]
        yc_ref[0, r0:r0 + rc, :] = yn * jax.nn.sigmoid(yn)

        for gi, o_ref in enumerate((hv_ref, hx1_ref, hx2_ref)):
            cs = slice(gi * HY_WIDTH, (gi + 1) * HY_WIDTH)
            hs = jnp.zeros((rc, HY_WIDTH), F32) + sb_ref[:, cs]
            for kk in range(HY_SHORT):
                hs = hs + s_scr[pl.ds(r0 + HALO - sp + kk, rc), cs] * sw_ref[kk:kk + 1, cs]
            if o_ref is hx2_ref:
                o_ref[0, r0:r0 + rc, :] = hs
            else:
                _store_slabs(o_ref, (0,), r0, hs)
    for o_ref in (hv_ref, hx1_ref):
        _zero_slab_pads(o_ref, (0,), t // DFT_MINOR)


def _local_call(loc, cw, cb, lg, lb, sw, sb, t=512):
    b, l, c = loc.shape
    t = min(t, l)
    r = t // HALO
    nh = l // HALO
    consts = [cw, cb, lg, lb, sw, sb]
    out_spec = pl.BlockSpec((1, t, HY_WIDTH), lambda bi, i: (bi, i, 0))
    out_shape = jax.ShapeDtypeStruct((b, l, HY_WIDTH), F32)
    slab_spec = pl.BlockSpec((1, _slab_rows(t), HY_WIDTH), lambda bi, i: (bi, i, 0))
    slab_shape = jax.ShapeDtypeStruct((b, _slab_rows(l), HY_WIDTH), F32)
    return pl.pallas_call(
        _local_body,
        grid=(b, l // t),
        in_specs=[pl.BlockSpec((1, t, c), lambda bi, i: (bi, i, 0)),
                  pl.BlockSpec((1, HALO, c), lambda bi, i: (bi, jnp.maximum(i * r - 1, 0), 0)),
                  pl.BlockSpec((1, HALO, c), lambda bi, i: (bi, jnp.minimum((i + 1) * r, nh - 1), 0))]
        + [_const_spec(a.shape) for a in consts],
        out_specs=[out_spec, slab_spec, slab_spec, out_spec],
        out_shape=[out_shape, slab_shape, slab_shape, out_shape],
        scratch_shapes=[pltpu.VMEM((t + 2 * HALO, CONV_WIDTH), F32),
                        pltpu.VMEM((t + 2 * HALO, 3 * HY_WIDTH), F32)],
        compiler_params=_cparams(("parallel", "parallel")),
        name="local_mixers",
    )(loc, loc, loc, *consts)


def _filter_body(z_ref, w1_ref, b1_ref, w2_ref, b2_ref, w3_ref, b3_ref, w4_ref, fr_ref, dl_ref,
                 kf_ref, kb_ref, s_ref):
    hp = lax.Precision.HIGHEST
    z = z_ref[...]
    fr = fr_ref[...]
    hd = jnp.sin(fr * (jnp.dot(z, w1_ref[...], precision=hp, preferred_element_type=F32) + b1_ref[...]))
    hd = jnp.sin(fr * (jnp.dot(hd, w2_ref[...], precision=hp, preferred_element_type=F32) + b2_ref[...]))
    hd = jnp.sin(fr * (jnp.dot(hd, w3_ref[...], precision=hp, preferred_element_type=F32) + b3_ref[...]))
    h = jnp.dot(hd, w4_ref[...], precision=hp, preferred_element_type=F32)
    tcol = z[:, 0:1]
    win = jnp.exp(-tcol * dl_ref[...])
    bwin = win * z[:, HY_EMB:HY_EMB + 1]

    @pl.when(pl.program_id(0) == 0)
    def _():
        s_ref[...] = jnp.zeros_like(s_ref)

    for o in range(HY_ORDER):
        asum = None
        for side, (k_ref, w) in enumerate(((kf_ref, win), (kb_ref, bwin))):
            c0 = (o * 2 + side) * HY_WIDTH
            ko = h[:, c0:c0 + HY_WIDTH] * w
            for r0 in range(0, ko.shape[0], DFT_MINOR):
                _store_slabs(k_ref, (o,), r0, ko[r0:r0 + DFT_MINOR])
            _zero_slab_pads(k_ref, (o,), ko.shape[0] // DFT_MINOR)
            part = jnp.sum(jnp.abs(ko), axis=0, keepdims=True)
            asum = part if asum is None else asum + part
        s_ref[o] += jnp.broadcast_to(asum, s_ref.shape[1:])


def _filter_call(zt, w1, b1, w2, b2, w3, b3, w4, fr, dl, t=512):
    l = zt.shape[0]
    t = min(t, l)
    consts = [w1, b1, w2, b2, w3, b3, w4, fr, dl]
    k_spec = pl.BlockSpec((HY_ORDER, _slab_rows(t), HY_WIDTH), lambda i: (0, i, 0))
    k_shape = jax.ShapeDtypeStruct((HY_ORDER, _slab_rows(l), HY_WIDTH), F32)
    return pl.pallas_call(
        _filter_body,
        grid=(l // t,),
        in_specs=[pl.BlockSpec((t, zt.shape[1]), lambda i: (i, 0))] + [_const_spec(a.shape) for a in consts],
        out_specs=[k_spec, k_spec, pl.BlockSpec((HY_ORDER, 8, HY_WIDTH), lambda i: (0, 0, 0))],
        out_shape=[k_shape, k_shape, jax.ShapeDtypeStruct((HY_ORDER, 8, HY_WIDTH), F32)],
        compiler_params=_cparams(("arbitrary",)),
        name="hyena_filter",
    )(zt, *consts)


def _dft_dims(l):
    n1 = 2 * l // DFT_MINOR
    nh = n1 // 2
    h = nh + 1
    hp = -(-h // 8) * 8
    return n1, nh, h, hp


def _unroll(trips):
    return max(u for u in range(1, DFT_UNROLL + 1) if trips % u == 0)


def _fwd_stage1(load_slab, tf1_ref, x1_scr, hp):
    def body(n2, c):
        u = load_slab(n2).astype(BF16)
        x1 = _dot(tf1_ref[n2], u)
        x1_scr[pl.ds(pl.multiple_of(n2 * _pitch(2 * hp), SUBLANES), 2 * hp), :] = x1
        return c
    lax.fori_loop(0, DFT_MINOR, body, 0, unroll=_unroll(DFT_MINOR))


def _stage2_operand(k1, x1_scr, hp):
    are = x1_scr[pl.ds(k1, DFT_MINOR, stride=_pitch(2 * hp)), :]
    aim = x1_scr[pl.ds(hp + k1, DFT_MINOR, stride=_pitch(2 * hp)), :]
    return jnp.concatenate([are, aim], axis=0).astype(BF16)


def _spectrum_body(kf_ref, kb_ref, s_ref, tf1_ref, gf_ref, o_ref, x1_scr, *, l):
    _, nh, h, hp = _dft_dims(l)
    inv = 1.0 / s_ref[0, 0:1, :]

    for half, ref in enumerate((kf_ref, kb_ref)):
        _fwd_stage1(lambda n2, ref=ref: ref[0, pl.ds(n2, nh, stride=SLAB_PITCH), :], tf1_ref, x1_scr, hp)

        def body(j, c, half=half):
            ks = (2 * j, 2 * j + 1)
            xx = _dot(gf_ref[...], jnp.concatenate([_stage2_operand(k1, x1_scr, hp) for k1 in ks], axis=1))
            m = DFT_MINOR
            cb = o_ref.shape[3]
            for i, k1 in enumerate(ks):
                x = xx[:, i * cb:(i + 1) * cb] * inv
                if half == 0:
                    o_ref[0, k1] = x
                else:
                    o_ref[0, k1, 0:m, :] = o_ref[0, k1, 0:m, :] + x[:m]
                    o_ref[0, k1, m:, :] = o_ref[0, k1, m:, :] - x[m:]
            return c
        lax.fori_loop(0, hp // 2, body, 0, unroll=_unroll(hp // 4))


def _spectrum_call(kfwd, kbwd, ssum, tf1, gf, l):
    _, nh, h, hp = _dft_dims(l)
    order, _, c = kfwd.shape
    cb = LANE
    return pl.pallas_call(
        functools.partial(_spectrum_body, l=l),
        grid=(order, c // cb),
        in_specs=[pl.BlockSpec((1, _slab_rows(l), cb), lambda o, ci: (o, 0, ci), pipeline_mode=pl.Buffered(1)),
                  pl.BlockSpec((1, _slab_rows(l), cb), lambda o, ci: (o, 0, ci), pipeline_mode=pl.Buffered(1)),
                  pl.BlockSpec((1, 8, cb), lambda o, ci: (o, 0, ci)),
                  _const_spec(tf1.shape), _const_spec(gf.shape)],
        out_specs=pl.BlockSpec((1, hp, 2 * DFT_MINOR, cb), lambda o, ci: (o, 0, 0, ci),
                               pipeline_mode=pl.Buffered(1)),
        out_shape=jax.ShapeDtypeStruct((order, hp, 2 * DFT_MINOR, c), F32),
        scratch_shapes=[pltpu.VMEM((DFT_MINOR * _pitch(2 * hp), cb), F32)],
        compiler_params=_cparams(("parallel", "parallel")),
        name="hyena_spectrum",
    )(kfwd, kbwd, ssum, tf1, gf)


def _longconv_body(*refs, gated, l):
    if gated:
        a_ref, b_ref, kf_ref, d_ref, tf1_ref, gf_ref, gi_ref, ti2_ref, o_ref, x1_scr, z_scr = refs
    else:
        a_ref, kf_ref, d_ref, tf1_ref, gf_ref, gi_ref, ti2_ref, o_ref, x1_scr, z_scr = refs
        b_ref = None
    _, nh, h, hp = _dft_dims(l)
    m = DFT_MINOR

    zp = _pitch(2 * m)

    def load_slab(n2):
        u = a_ref[0, pl.ds(n2, nh, stride=SLAB_PITCH), :]
        if gated:
            u = u * b_ref[0, pl.ds(n2, nh, stride=SLAB_PITCH), :]
        return u

    _fwd_stage1(load_slab, tf1_ref, x1_scr, hp)

    cb = o_ref.shape[2]

    def mid(j, c):
        ks = (2 * j, 2 * j + 1)
        x = _dot(gf_ref[...], jnp.concatenate([_stage2_operand(k1, x1_scr, hp) for k1 in ks], axis=1))
        xre, xim = x[:m], x[m:]
        kre = jnp.concatenate([kf_ref[0, k1, 0:m, :] for k1 in ks], axis=1)
        kim = jnp.concatenate([kf_ref[0, k1, m:2 * m, :] for k1 in ks], axis=1)
        y = jnp.concatenate([xre * kre - xim * kim, xre * kim + xim * kre], axis=0).astype(BF16)
        z = _dot(gi_ref[...], y)
        for i, k1 in enumerate(ks):
            z_scr[pl.ds(pl.multiple_of(k1 * zp, SUBLANES), 2 * m), :] = z[:, i * cb:(i + 1) * cb]
        return c
    lax.fori_loop(0, hp // 2, mid, 0, unroll=_unroll(hp // 2))

    def last(n2, c):
        zre = z_scr[pl.ds(n2, hp, stride=zp), :]
        zim = z_scr[pl.ds(m + n2, hp, stride=zp), :]
        rhs = jnp.concatenate([zre, zim], axis=0).astype(BF16)
        y = _dot(ti2_ref[n2], rhs)
        o_ref[0, pl.ds(n2, nh, stride=SLAB_PITCH), :] = y
        return c
    lax.fori_loop(0, m, last, 0, unroll=_unroll(m))

    def epi(j, c):
        sl = pl.ds(pl.multiple_of(j * SLAB_PITCH, SUBLANES), m)
        u = a_ref[0, sl, :]
        if gated:
            u = u * b_ref[0, sl, :]
        o_ref[0, sl, :] = o_ref[0, sl, :] + u * d_ref[0]
        pad = pl.ds(pl.multiple_of(j * SLAB_PITCH + m, SUBLANES), SLAB_PITCH - m)
        o_ref[0, pad, :] = jnp.zeros((SLAB_PITCH - m, o_ref.shape[2]), F32)
        return c
    lax.fori_loop(0, nh, epi, 0, unroll=_unroll(nh))


def _longconv_call(a, b, kf, d, tf1, gf, gi, ti2, order, l):
    bn, _, c = a.shape
    _, nh, h, hp = _dft_dims(l)
    cb = LANE
    sig = pl.BlockSpec((1, _slab_rows(l), cb), lambda ci, bi: (bi, 0, ci), pipeline_mode=pl.Buffered(1))
    args = [a] + ([b] if b is not None else [])
    specs = [sig] * len(args)
    args += [kf, d, tf1, gf, gi, ti2]
    specs += [pl.BlockSpec((1, hp, 2 * DFT_MINOR, cb), lambda ci, bi: (order, 0, 0, ci),
                           pipeline_mode=pl.Buffered(1)),
              pl.BlockSpec((1, 1, cb), lambda ci, bi: (order, 0, ci)),
              _const_spec(tf1.shape), _const_spec(gf.shape), _const_spec(gi.shape), _const_spec(ti2.shape)]
    return pl.pallas_call(
        functools.partial(_longconv_body, gated=b is not None, l=l),
        grid=(c // cb, bn),
        in_specs=specs,
        out_specs=pl.BlockSpec((1, _slab_rows(l), cb), lambda ci, bi: (bi, 0, ci)),
        out_shape=jax.ShapeDtypeStruct((bn, _slab_rows(l), c), F32),
        scratch_shapes=[pltpu.VMEM((DFT_MINOR * _pitch(2 * hp), cb), F32),
                        pltpu.VMEM((hp * _pitch(2 * DFT_MINOR), cb), F32)],
        compiler_params=_cparams(("parallel", "parallel")),
        name="hyena_longconv",
    )(*args)


@functools.lru_cache(maxsize=None)
def _dft_tables(l):
    n = 2 * l
    n1, nh, h, hp = _dft_dims(l)
    m = DFT_MINOR
    n2i = np.arange(m)[:, None, None]
    k1i = np.arange(h)[None, :, None]
    n1i = np.arange(nh)[None, None, :]
    ang = 2.0 * np.pi * ((k1i * (m * n1i + n2i)) % n) / n
    tf1 = np.zeros((m, 2 * hp, nh), np.float32)
    tf1[:, :h] = np.cos(ang)
    tf1[:, hp:hp + h] = -np.sin(ang)
    w = np.full((h,), 2.0)
    w[0] = 1.0
    w[-1] = 1.0
    ti2 = np.zeros((m, nh, 2 * hp), np.float32)
    ti2[:, :, :h] = np.transpose(np.cos(ang) * (w[None, :, None] / n), (0, 2, 1))
    ti2[:, :, hp:hp + h] = np.transpose(-np.sin(ang) * (w[None, :, None] / n), (0, 2, 1))
    th = 2.0 * np.pi * ((np.arange(m)[:, None] * np.arange(m)[None, :]) % m) / m
    cm, sm = np.cos(th), np.sin(th)
    gf = np.block([[cm, sm], [-sm, cm]]).astype(np.float32)
    gi = np.block([[cm, -sm], [sm, cm]]).astype(np.float32)
    return tf1, gf, gi, ti2


@functools.lru_cache(maxsize=None)
def _filter_tables(l):
    bands = (HY_EMB - 1) // 2
    t = np.linspace(0.0, 1.0, l)[:, None]
    ang = 2.0 * np.pi * np.arange(l)[:, None] / l
    fb = np.linspace(1e-4, bands - 1, bands)[None, :]
    z = np.concatenate([t, np.cos(fb * ang), -np.sin(fb * ang)], axis=-1)
    zt = np.zeros((l, LANE), np.float32)
    zt[:, :HY_EMB] = z
    zt[:, HY_EMB] = 1.0
    zt[0, HY_EMB] = 0.0
    max_decay = math.log(HY_TARGET) / HY_FAST
    min_decay = math.log(HY_TARGET) / HY_SLOW
    deltas = np.abs(np.linspace(min_decay, max_decay, HY_WIDTH))[None, :].astype(np.float32)
    return zt, deltas


def _rope_rows():
    inv_freq = 1.0 / (ROPE_THETA ** (np.arange(0, ROPE, 2, dtype=np.float32) / ROPE))
    half = ROPE // 2
    freq = np.zeros((1, HEAD_PAD), np.float32)
    lo = np.zeros((1, HEAD_PAD), np.float32)
    hi = np.zeros((1, HEAD_PAD), np.float32)
    freq[0, NOPE:NOPE + half] = inv_freq
    freq[0, NOPE + half:NOPE + ROPE] = inv_freq
    lo[0, NOPE:NOPE + half] = -1.0
    hi[0, NOPE + half:NOPE + ROPE] = 1.0
    vone = np.zeros((1, HEAD_PAD), np.float32)
    vone[0, VDIM] = 1.0
    return freq, lo, hi, vone


def _pad_cols(w, width, at=0):
    out = jnp.zeros(w.shape[:-1] + (width,), w.dtype)
    return out.at[..., at:at + w.shape[-1]].set(w)


def _pad2(w, rows, cols):
    out = jnp.zeros((rows, cols), w.dtype)
    return out.at[:w.shape[0], :w.shape[1]].set(w)


def kernel(x, positions, ffn1_norm, ffn1_w_gate, ffn1_w_up, ffn1_w_down, mix_norm, w_in, mla_q_norm, mla_w_qb, mla_kv_norm, mla_w_kvb, conv_dw_w, conv_dw_b, conv_ln_g, conv_ln_b, hy_short_w, hy_short_b, hy_filt_w1, hy_filt_b1, hy_filt_w2, hy_filt_b2, hy_filt_w3, hy_filt_b3, hy_filt_w4, hy_filt_freq, hy_bias_d, out_norm, w_out, ffn2_norm, ffn2_w_gate, ffn2_w_up, ffn2_w_down, final_norm):
    b, l, d = x.shape
    n = b * l
    depth = w_in.shape[0]
    row = lambda v: v.reshape(1, -1).astype(F32)

    tf1, gf, gi, ti2 = (jnp.asarray(t).astype(BF16) for t in _dft_tables(l))
    zt_np, deltas_np = _filter_tables(l)
    zt, deltas = jnp.asarray(zt_np), jnp.asarray(deltas_np)
    freq, lo, hi, vone = (jnp.asarray(t) for t in _rope_rows())
    pos = positions.astype(F32).reshape(n, 1)

    f1g, f1u, f1d, f2g, f2u, f2d, wo = (_cast_call(w) for w in (
        ffn1_w_gate, ffn1_w_up, ffn1_w_down, ffn2_w_gate, ffn2_w_up, ffn2_w_down, w_out))

    xs = x.reshape(n, d)
    for i in range(depth):
        xs = _ffn_call(xs, row(ffn1_norm[i]), f1g, f1u, f1d, i)

        wi = w_in[i]
        win = jnp.concatenate([
            wi[:, OFF_Q:OFF_KV], wi[:, OFF_KV:OFF_KPE], _pad_cols(wi[:, OFF_KPE:OFF_CONV], HEAD_PAD, NOPE),
            wi[:, OFF_CONV:]], axis=1).astype(BF16)
        wq = _pad_cols(mla_w_qb[i].reshape(Q_RANK, HEADS, QK_DIM), HEAD_PAD)
        wq = wq.reshape(Q_RANK, HEADS * HEAD_PAD).astype(BF16)
        wkv = mla_w_kvb[i].reshape(KV_RANK, HEADS, NOPE + VDIM)
        wk = _pad_cols(wkv[..., :NOPE], HEAD_PAD).reshape(KV_RANK, HEADS * HEAD_PAD).astype(BF16)
        wv = _pad_cols(wkv[..., NOPE:], HEAD_PAD).reshape(KV_RANK, HEADS * HEAD_PAD).astype(BF16)

        q, k, v, loc = _proj_call(xs, pos, b, l, row(mix_norm[i]), win, row(mla_q_norm[i]), wq,
                                  row(mla_kv_norm[i]), wk, wv, freq, lo, hi, vone)
        y_mla = _attn_call(q, k, v).reshape(n, MLA_WIDTH)

        y_conv, hv, hx1, hx2 = _local_call(
            loc.reshape(b, l, LOC_COLS), conv_dw_w[i], row(conv_dw_b[i]), row(conv_ln_g[i]),
            row(conv_ln_b[i]), hy_short_w[i], row(hy_short_b[i]))

        hrow = lambda v_: _pad2(v_.reshape(1, -1), 1, LANE)
        kfwd, kbwd, ssum = _filter_call(
            zt, _pad2(hy_filt_w1[i], LANE, LANE), hrow(hy_filt_b1[i]), _pad2(hy_filt_w2[i], LANE, LANE),
            hrow(hy_filt_b2[i]), _pad2(hy_filt_w3[i], LANE, LANE), hrow(hy_filt_b3[i]),
            _pad2(hy_filt_w4[i], LANE, HY_ORDER * 2 * HY_WIDTH), hrow(hy_filt_freq[i]), deltas)
        kf = _spectrum_call(kfwd, kbwd, ssum, tf1, gf, l)
        dd = hy_bias_d[i].reshape(HY_ORDER, 1, HY_WIDTH)
        y1 = _longconv_call(hv, None, kf, dd, tf1, gf, gi, ti2, 0, l)
        y2 = _longconv_call(hx1, y1, kf, dd, tf1, gf, gi, ti2, 1, l)

        last = i == depth - 1
        xs = _ffn_call(xs, row(ffn2_norm[i]), f2g, f2u, f2d, i,
                       mix=(y_mla, y_conv.reshape(n, -1), y2.reshape(-1, HY_WIDTH), hx2.reshape(n, -1),
                            row(out_norm[i]), wo),
                       final_g=row(final_norm) if last else None)
    return xs.reshape(b, l, d)
```
